```python
import jax
import jax.numpy as jnp
from jax import lax
import numpy as np

D_MODEL = 2048
BATCH = 16
SEQ = 2048
DEPTH = 1
DEC_BATCH = 128
DEC_SEQ = 1
PAST_LEN = 16384
PAGE_SIZE = 128

HEAD_DIM = 64
N_HEADS = 16
N_KV_HEADS = 4
GROUP = N_HEADS // N_KV_HEADS
WINDOW = 128
BLOCK = WINDOW
ROT_DIM = HEAD_DIM // 4
ROPE_THETA = 500000.0
ATTN_SCALE = HEAD_DIM ** -0.5
Q_DIM = N_HEADS * HEAD_DIM
KV_DIM = N_KV_HEADS * HEAD_DIM
CONV_CH = D_MODEL - Q_DIM
CONV_W = 31
MIX_WIDTH = Q_DIM + CONV_CH
IN_DIM = Q_DIM + 2 * KV_DIM + 2 * CONV_CH
D_FF = 4 * D_MODEL
EPS = 1e-5

kernel_name = 'hymba_swa_sink_conformer_conv_sqrelu_step'


def _rms_norm(x, g):
    xf = x.astype(jnp.float32)
    y = xf * lax.rsqrt(jnp.mean(xf * xf, axis=-1, keepdims=True) + EPS)
    return (y * g.astype(jnp.float32)).astype(x.dtype)


def _partial_rope(x, pos):
    half = ROT_DIM // 2
    inv_freq = jnp.power(jnp.float32(ROPE_THETA), -jnp.arange(half, dtype=jnp.float32) * 2.0 / ROT_DIM)
    ang = pos.astype(jnp.float32)[:, None] * inv_freq[None, :]
    cos = jnp.cos(ang)[None, :, None, :]
    sin = jnp.sin(ang)[None, :, None, :]
    xr = x[..., :ROT_DIM].astype(jnp.float32)
    x1, x2 = xr[..., :half], xr[..., half:]
    rot = jnp.concatenate([x1 * cos - x2 * sin, x2 * cos + x1 * sin], axis=-1).astype(x.dtype)
    return jnp.concatenate([rot, x[..., ROT_DIM:]], axis=-1)


def _project_in(hn, w_in, b_in, pos):
    n, t, _ = hn.shape
    z = hn @ w_in + b_in
    o1, o2, o3, o4 = Q_DIM, Q_DIM + KV_DIM, Q_DIM + 2 * KV_DIM, Q_DIM + 2 * KV_DIM + CONV_CH
    q = z[..., :o1].reshape(n, t, N_HEADS, HEAD_DIM)
    k = z[..., o1:o2].reshape(n, t, N_KV_HEADS, HEAD_DIM)
    v = z[..., o2:o3].reshape(n, t, N_KV_HEADS, HEAD_DIM)
    u = z[..., o3:o4] * jax.nn.sigmoid(z[..., o4:])
    return _partial_rope(q, pos), _partial_rope(k, pos), v, u


def _sink_softmax(s, mask, sink):
    s = jnp.where(mask, s, -jnp.inf)
    m = jnp.maximum(jnp.max(s, axis=-1, keepdims=True), sink)
    p = jnp.exp(s - m)
    return p / (jnp.sum(p, axis=-1, keepdims=True) + jnp.exp(sink - m))


def _banded_window_attention(q, k, v, sinks):
    n, t = q.shape[:2]
    nb = t // BLOCK
    qb = q.reshape(n, nb, BLOCK, N_KV_HEADS, GROUP, HEAD_DIM)
    kb = k.reshape(n, nb, BLOCK, N_KV_HEADS, HEAD_DIM)
    vb = v.reshape(n, nb, BLOCK, N_KV_HEADS, HEAD_DIM)

    def with_prev(xb):
        prev = jnp.concatenate([jnp.zeros_like(xb[:, :1]), xb[:, :-1]], axis=1)
        return jnp.concatenate([prev, xb], axis=2)

    kk, vv = with_prev(kb), with_prev(vb)
    s = jnp.einsum('bnqhgd,bnshd->bnhgqs', qb, kk, preferred_element_type=jnp.float32) * ATTN_SCALE
    i = jnp.arange(BLOCK)[:, None]
    j = jnp.arange(2 * BLOCK)[None, :]
    rel = i + BLOCK - j
    band = (rel >= 0) & (rel < WINDOW)
    has_prev = (jnp.arange(nb) > 0)[:, None, None] | (j >= BLOCK)[None]
    mask = (band[None] & has_prev)[None, :, None, None]
    sink = sinks.astype(jnp.float32).reshape(1, 1, N_KV_HEADS, GROUP, 1, 1)
    p = _sink_softmax(s, mask, sink)
    o = jnp.einsum('bnhgqs,bnshd->bnqhgd', p.astype(v.dtype), vv)
    return o.reshape(n, t, Q_DIM)


def _window_cache_attention(q, k, v, buf_k, buf_v, sinks):
    n, t = q.shape[:2]
    wb = buf_k.shape[1]
    kk = jnp.concatenate([buf_k.astype(k.dtype), k], axis=1)
    vv = jnp.concatenate([buf_v.astype(v.dtype), v], axis=1)
    qg = q.reshape(n, t, N_KV_HEADS, GROUP, HEAD_DIM)
    s = jnp.einsum('bqhgd,bshd->bhgqs', qg, kk, preferred_element_type=jnp.float32) * ATTN_SCALE
    q_pos = PAST_LEN + jnp.arange(t)
    k_pos = PAST_LEN - wb + jnp.arange(wb + t)
    rel = q_pos[:, None] - k_pos[None, :]
    mask = ((rel >= 0) & (rel < WINDOW))[None, None, None]
    sink = sinks.astype(jnp.float32).reshape(1, N_KV_HEADS, GROUP, 1, 1)
    p = _sink_softmax(s, mask, sink)
    o = jnp.einsum('bhgqs,bshd->bqhgd', p.astype(vv.dtype), vv)
    return o.reshape(n, t, Q_DIM), kk[:, -wb:], vv[:, -wb:]


def _conv_branch(u_ext, conv_w, conv_b, ln_g, ln_b):
    y = lax.conv_general_dilated(
        u_ext, conv_w[:, None, :].astype(u_ext.dtype), window_strides=(1,), padding='VALID',
        dimension_numbers=('NWC', 'WIO', 'NWC'), feature_group_count=u_ext.shape[-1])
    y = (y + conv_b).astype(jnp.float32)
    mu = jnp.mean(y, axis=-1, keepdims=True)
    yc = y - mu
    var = jnp.mean(yc * yc, axis=-1, keepdims=True)
    yn = yc * lax.rsqrt(var + EPS) * ln_g.astype(jnp.float32) + ln_b.astype(jnp.float32)
    return (yn * jax.nn.sigmoid(yn)).astype(u_ext.dtype)


def _merge_and_mlp(h, attn_o, conv_o, w_out, b_out, norm2_g, w_up, w_down):
    h = h + jnp.concatenate([attn_o, conv_o], axis=-1) @ w_out + b_out
    hn = _rms_norm(h, norm2_g)
    return h + jnp.square(jax.nn.relu(hn @ w_up)) @ w_down


def setup_inputs(seed: int = 0) -> dict:
    key = jax.random.key(seed)
    ks = jax.random.split(key, 20)
    f32 = jnp.float32
    wb = min(WINDOW, PAST_LEN)
    nrm = lambda k, shape, scale: jax.random.normal(k, shape, f32) * scale
    return {
        'x_prompt': nrm(ks[0], (BATCH, SEQ, D_MODEL), 1.0),
        'x_sample': nrm(ks[1], (DEC_BATCH, DEC_SEQ, D_MODEL), 1.0),
        'cache_k': nrm(ks[2], (DEPTH, DEC_BATCH, wb, N_KV_HEADS, HEAD_DIM), 1.0),
        'cache_v': nrm(ks[3], (DEPTH, DEC_BATCH, wb, N_KV_HEADS, HEAD_DIM), 1.0),
        'cache_conv': nrm(ks[4], (DEPTH, DEC_BATCH, CONV_W - 1, CONV_CH), 0.5),
        'norm1_g': 1.0 + nrm(ks[5], (DEPTH, D_MODEL), 0.02),
        'w_in': nrm(ks[6], (DEPTH, D_MODEL, IN_DIM), D_MODEL ** -0.5),
        'b_in': nrm(ks[7], (DEPTH, IN_DIM), 0.02),
        'attn_sinks': nrm(ks[8], (DEPTH, N_HEADS), 0.5),
        'conv_w': nrm(ks[9], (DEPTH, CONV_W, CONV_CH), CONV_W ** -0.5),
        'conv_b': nrm(ks[10], (DEPTH, CONV_CH), 0.02),
        'conv_ln_g': 1.0 + nrm(ks[11], (DEPTH, CONV_CH), 0.02),
        'conv_ln_b': nrm(ks[12], (DEPTH, CONV_CH), 0.02),
        'w_out': nrm(ks[13], (DEPTH, MIX_WIDTH, D_MODEL), MIX_WIDTH ** -0.5),
        'b_out': nrm(ks[14], (DEPTH, D_MODEL), 0.02),
        'norm2_g': 1.0 + nrm(ks[15], (DEPTH, D_MODEL), 0.02),
        'w_up': nrm(ks[16], (DEPTH, D_MODEL, D_FF), D_MODEL ** -0.5),
        'w_down': nrm(ks[17], (DEPTH, D_FF, D_MODEL), D_FF ** -0.5),
        'final_norm_g': 1.0 + nrm(ks[18], (D_MODEL,), 0.02),
    }


def reference(x_prompt, x_sample, cache_k, cache_v, cache_conv, norm1_g, w_in, b_in, attn_sinks,
              conv_w, conv_b, conv_ln_g, conv_ln_b, w_out, b_out, norm2_g, w_up, w_down, final_norm_g):
    t_p = x_prompt.shape[1]
    t_s = x_sample.shape[1]
    pos_p = jnp.arange(t_p, dtype=jnp.int32)
    pos_s = PAST_LEN + jnp.arange(t_s, dtype=jnp.int32)
    wp = min(WINDOW, t_p)
    hp, hs = x_prompt, x_sample
    pk, pv, pc, sk, sv, sc = [], [], [], [], [], []
    for l in range(DEPTH):
        q, k, v, u = _project_in(_rms_norm(hp, norm1_g[l]), w_in[l], b_in[l], pos_p)
        a_o = _banded_window_attention(q, k, v, attn_sinks[l])
        c_o = _conv_branch(jnp.pad(u, ((0, 0), (CONV_W - 1, 0), (0, 0))),
                           conv_w[l], conv_b[l], conv_ln_g[l], conv_ln_b[l])
        hp = _merge_and_mlp(hp, a_o, c_o, w_out[l], b_out[l], norm2_g[l], w_up[l], w_down[l])
        pk.append(k[:, -wp:])
        pv.append(v[:, -wp:])
        pc.append(u[:, -(CONV_W - 1):])
        q, k, v, u = _project_in(_rms_norm(hs, norm1_g[l]), w_in[l], b_in[l], pos_s)
        a_o, nk, nv = _window_cache_attention(q, k, v, cache_k[l], cache_v[l], attn_sinks[l])
        u_ext = jnp.concatenate([cache_conv[l].astype(u.dtype), u], axis=1)
        c_o = _conv_branch(u_ext, conv_w[l], conv_b[l], conv_ln_g[l], conv_ln_b[l])
        hs = _merge_and_mlp(hs, a_o, c_o, w_out[l], b_out[l], norm2_g[l], w_up[l], w_down[l])
        sk.append(nk)
        sv.append(nv)
        sc.append(u_ext[:, -(CONV_W - 1):])
    y_prompt = _rms_norm(hp, final_norm_g)
    y_sample = _rms_norm(hs, final_norm_g)
    return (y_prompt, y_sample, jnp.stack(pk), jnp.stack(pv), jnp.stack(pc),
            jnp.stack(sk), jnp.stack(sv), jnp.stack(sc))
```

```python
import functools

import jax
import jax.numpy as jnp
import numpy as np
from jax import lax
from jax.experimental import pallas as pl
from jax.experimental.pallas import tpu as pltpu

HEAD_DIM = 64
N_HEADS = 16
N_KV_HEADS = 4
GROUP = N_HEADS // N_KV_HEADS
WINDOW = 128
ROT_DIM = HEAD_DIM // 4
ROPE_THETA = 500000.0
ATTN_SCALE = HEAD_DIM ** -0.5
Q_DIM = N_HEADS * HEAD_DIM
KV_DIM = N_KV_HEADS * HEAD_DIM
CONV_W = 31
PAST_LEN = 16384
EPS = 1e-5

LANES = 128
CONV_HALO = 32
VMEM_LIMIT_BYTES = 56 * 1024 * 1024

BF16 = jnp.bfloat16
F32 = jnp.float32


def _resident(shape):
    return pl.BlockSpec(shape, lambda *_: (0,) * len(shape), pipeline_mode=pl.Buffered(1))


def _rms_norm_f32(x, g):
    return x * lax.rsqrt(jnp.mean(x * x, axis=-1, keepdims=True) + EPS) * g


def _dot(a, b):
    return jnp.dot(a, b, preferred_element_type=F32)


def _dot_nt(a, b):
    return lax.dot_general(a, b, (((1,), (1,)), ((), ())), preferred_element_type=F32)


def _rope(z, cos, sin_up, sin_dn):
    cols = []
    for c in range(z.shape[1] // LANES):
        zc = z[:, c * LANES:(c + 1) * LANES]
        cols.append(zc * cos + pltpu.roll(zc, ROT_DIM // 2, 1) * sin_up
                    + pltpu.roll(zc, LANES - ROT_DIM // 2, 1) * sin_dn)
    return jnp.concatenate(cols, axis=1)


def _in_proj_kernel(x_ref, g_ref, w_ref, b_ref, cos_ref, sup_ref, sdn_ref,
                    q_ref, k_ref, v_ref, u_ref):
    conv_ch = u_ref.shape[1]
    o1, o2, o3, o4 = Q_DIM, Q_DIM + KV_DIM, Q_DIM + 2 * KV_DIM, Q_DIM + 2 * KV_DIM + conv_ch
    hn = _rms_norm_f32(x_ref[...], g_ref[...]).astype(BF16)
    cos, sup, sdn = cos_ref[...], sup_ref[...], sdn_ref[...]

    zq = _dot(hn, w_ref[:, 0:o1]) + b_ref[:, 0:o1]
    q_ref[...] = (_rope(zq, cos, sup, sdn) * ATTN_SCALE).astype(q_ref.dtype)
    zk = _dot(hn, w_ref[:, o1:o2]) + b_ref[:, o1:o2]
    k_ref[...] = _rope(zk, cos, sup, sdn)
    v_ref[...] = _dot(hn, w_ref[:, o2:o3]) + b_ref[:, o2:o3]
    zu = _dot(hn, w_ref[:, o3:o4]) + b_ref[:, o3:o4]
    zg = _dot(hn, w_ref[:, o4:]) + b_ref[:, o4:]
    u_ref[...] = zu * jax.nn.sigmoid(zg)


def _in_proj(x, g, w, b, cos, sup, sdn, tm, q_dtype):
    m, d = x.shape
    n = w.shape[1]
    conv_ch = (n - Q_DIM - 2 * KV_DIM) // 2
    nt = cos.shape[0] // tm
    row = lambda i: (i, 0)
    tab = lambda i: (i % nt, 0)
    return pl.pallas_call(
        _in_proj_kernel,
        grid=(m // tm,),
        in_specs=[pl.BlockSpec((tm, d), row), _resident((1, d)), _resident((d, n)), _resident((1, n)),
                  pl.BlockSpec((tm, LANES), tab), pl.BlockSpec((tm, LANES), tab),
                  pl.BlockSpec((tm, LANES), tab)],
        out_specs=[pl.BlockSpec((tm, Q_DIM), row), pl.BlockSpec((tm, KV_DIM), row),
                   pl.BlockSpec((tm, KV_DIM), row), pl.BlockSpec((tm, conv_ch), row)],
        out_shape=[jax.ShapeDtypeStruct((m, Q_DIM), q_dtype), jax.ShapeDtypeStruct((m, KV_DIM), F32),
                   jax.ShapeDtypeStruct((m, KV_DIM), F32), jax.ShapeDtypeStruct((m, conv_ch), F32)],
        compiler_params=pltpu.CompilerParams(dimension_semantics=("arbitrary",),
                                             vmem_limit_bytes=VMEM_LIMIT_BYTES),
        name="in_proj",
    )(x, g, w, b, cos, sup, sdn)


def _group_mask(rows):
    lane_group = lax.broadcasted_iota(jnp.int32, (rows, KV_DIM), 1) // HEAD_DIM
    return [lane_group == g for g in range(N_KV_HEADS)]


def _layer_norm_swish(y, g, b):
    mu = jnp.mean(y, axis=-1, keepdims=True)
    yc = y - mu
    var = jnp.mean(yc * yc, axis=-1, keepdims=True)
    yn = yc * lax.rsqrt(var + EPS) * g + b
    return yn * jax.nn.sigmoid(yn)


def _mix_prompt_kernel(q_ref, k_ref, kp_ref, v_ref, vp_ref, u_ref, up_ref, sink_ref,
                       cw_ref, cb_ref, lg_ref, lb_ref, ao_ref, co_ref, ue_ref):
    j = pl.program_id(1)
    blk = q_ref.shape[0]
    has_prev = j > 0

    kk = jnp.concatenate([kp_ref[...], k_ref[...]], axis=0).astype(BF16)
    vv = jnp.concatenate([vp_ref[...], v_ref[...]], axis=0).astype(BF16)
    gmask = _group_mask(2 * blk)
    zero = jnp.zeros_like(kk)
    kstack = jnp.concatenate([jnp.where(gm, kk, zero) for gm in gmask], axis=0)
    vstack = jnp.concatenate([jnp.where(gm, vv, zero) for gm in gmask], axis=0)

    qi = lax.broadcasted_iota(jnp.int32, (blk, 2 * blk), 0)
    kj = lax.broadcasted_iota(jnp.int32, (blk, 2 * blk), 1)
    rel = qi + blk - kj
    valid = (rel >= 0) & (rel < WINDOW) & (has_prev | (kj >= blk))

    for i in range(GROUP):
        s_all = _dot_nt(q_ref[:, i * KV_DIM:(i + 1) * KV_DIM], kstack)
        probs = []
        for g in range(N_KV_HEADS):
            sink = sink_ref[g * GROUP + i]
            s = jnp.where(valid, s_all[:, g * 2 * blk:(g + 1) * 2 * blk], -jnp.inf)
            m = jnp.maximum(jnp.max(s, axis=-1, keepdims=True), sink)
            p = jnp.exp(s - m)
            denom = jnp.sum(p, axis=-1, keepdims=True) + jnp.exp(sink - m)
            probs.append((p / denom).astype(BF16))
        ao_ref[:, i * KV_DIM:(i + 1) * KV_DIM] = _dot(
            jnp.concatenate(probs, axis=1), vstack).astype(ao_ref.dtype)

    halo = up_ref[...]
    ue_ref[0:CONV_HALO, :] = jnp.where(has_prev, halo, jnp.zeros_like(halo))
    ue_ref[CONV_HALO:, :] = u_ref[...]
    first = CONV_HALO - (CONV_W - 1)
    for c in range(u_ref.shape[1] // LANES):
        cs = slice(c * LANES, (c + 1) * LANES)
        acc = jnp.zeros((blk, LANES), F32)
        for t in range(CONV_W):
            acc = acc + cw_ref[t:t + 1, cs] * ue_ref[first + t:first + t + blk, cs]
        ue_ref[CONV_HALO:, cs] = acc + cb_ref[:, cs]
    co_ref[...] = _layer_norm_swish(ue_ref[CONV_HALO:, :], lg_ref[...], lb_ref[...]).astype(co_ref.dtype)


def _mix_prompt(q, k, v, u, sinks, conv_w, conv_b, ln_g, ln_b, batch, seq):
    m = q.shape[0]
    conv_ch = u.shape[1]
    blk = WINDOW
    nb = seq // blk
    hpb = blk // CONV_HALO
    own = lambda b, j: (b * nb + j, 0)
    prev = lambda b, j: (b * nb + jnp.maximum(j - 1, 0), 0)
    halo = lambda b, j: (jnp.maximum((b * nb + j) * hpb - 1, 0), 0)
    return pl.pallas_call(
        _mix_prompt_kernel,
        grid=(batch, nb),
        in_specs=[pl.BlockSpec((blk, Q_DIM), own),
                  pl.BlockSpec((blk, KV_DIM), own), pl.BlockSpec((blk, KV_DIM), prev),
                  pl.BlockSpec((blk, KV_DIM), own), pl.BlockSpec((blk, KV_DIM), prev),
                  pl.BlockSpec((blk, conv_ch), own), pl.BlockSpec((CONV_HALO, conv_ch), halo),
                  pl.BlockSpec(memory_space=pltpu.SMEM),
                  _resident((CONV_W, conv_ch)), _resident((1, conv_ch)),
                  _resident((1, conv_ch)), _resident((1, conv_ch))],
        out_specs=[pl.BlockSpec((blk, Q_DIM), own), pl.BlockSpec((blk, conv_ch), own)],
        out_shape=[jax.ShapeDtypeStruct((m, Q_DIM), BF16), jax.ShapeDtypeStruct((m, conv_ch), BF16)],
        scratch_shapes=[pltpu.VMEM((CONV_HALO + blk, conv_ch), F32)],
        compiler_params=pltpu.CompilerParams(dimension_semantics=("arbitrary", "arbitrary"),
                                             vmem_limit_bytes=VMEM_LIMIT_BYTES),
        name="mix_prompt",
    )(q, k, k, v, v, u, u, sinks, conv_w, conv_b, ln_g, ln_b)


def _mix_sample_kernel(q_ref, k_ref, v_ref, u_ref, ck_ref, cv_ref, cc_ref, sink_ref,
                       cw_ref, cb_ref, lg_ref, lb_ref,
                       ao_ref, co_ref, nk_ref, nv_ref, nc_ref):
    nseq, wb, _ = ck_ref.shape
    hist = cc_ref.shape[1]
    rows = N_HEADS
    lane_group = lax.broadcasted_iota(jnp.int32, (rows, KV_DIM), 1) // HEAD_DIM
    row_group = lax.broadcasted_iota(jnp.int32, (rows, KV_DIM), 0) % N_KV_HEADS
    own_lanes = lane_group == row_group
    fold = (lax.broadcasted_iota(jnp.int32, (8, rows), 1) // N_KV_HEADS
            == lax.broadcasted_iota(jnp.int32, (8, rows), 0)).astype(F32)
    last = lax.broadcasted_iota(jnp.int32, (wb, KV_DIM), 0) == wb - 1
    sink = sink_ref[...]

    for n in range(nseq):
        knew = jnp.where(last, k_ref[n:n + 1, :], pltpu.roll(ck_ref[n], wb - 1, 0))
        vnew = jnp.where(last, v_ref[n:n + 1, :], pltpu.roll(cv_ref[n], wb - 1, 0))
        nk_ref[n] = knew
        nv_ref[n] = vnew
        qrows = jnp.where(own_lanes, q_ref[n], jnp.zeros((rows, KV_DIM), F32))
        s = _dot_nt(qrows.astype(BF16), knew.astype(BF16))
        m = jnp.maximum(jnp.max(s, axis=-1, keepdims=True), sink)
        p = jnp.exp(s - m)
        denom = jnp.sum(p, axis=-1, keepdims=True) + jnp.exp(sink - m)
        o = _dot((p / denom).astype(BF16), vnew.astype(BF16))
        o = jnp.where(own_lanes, o, jnp.zeros_like(o))
        ao_ref[n] = _dot(fold, o)

        cc = cc_ref[n]
        urow = u_ref[n:n + 1, :]
        y = (jnp.sum(cc * cw_ref[0:hist, :], axis=0, keepdims=True)
             + urow * cw_ref[hist:hist + 1, :] + cb_ref[...])
        co_ref[n:n + 1, :] = _layer_norm_swish(y, lg_ref[...], lb_ref[...])
        nc_ref[n, 0:hist - 1, :] = cc_ref[n, 1:hist, :]
        nc_ref[n, hist - 1:hist, :] = urow


def _mix_sample(q_rep, k, v, u, cache_k, cache_v, cache_conv, sink_rows, conv_w, conv_b, ln_g, ln_b, nseq):
    n, wb, _ = cache_k.shape
    hist, conv_ch = cache_conv.shape[1:]
    r2 = lambda s: (s, 0)
    r3 = lambda s: (s, 0, 0)
    return pl.pallas_call(
        _mix_sample_kernel,
        grid=(n // nseq,),
        in_specs=[pl.BlockSpec((nseq, N_HEADS, KV_DIM), r3),
                  pl.BlockSpec((nseq, KV_DIM), r2), pl.BlockSpec((nseq, KV_DIM), r2),
                  pl.BlockSpec((nseq, conv_ch), r2),
                  pl.BlockSpec((nseq, wb, KV_DIM), r3), pl.BlockSpec((nseq, wb, KV_DIM), r3),
                  pl.BlockSpec((nseq, hist, conv_ch), r3),
                  _resident((N_HEADS, 1)),
                  _resident((CONV_W, conv_ch)), _resident((1, conv_ch)),
                  _resident((1, conv_ch)), _resident((1, conv_ch))],
        out_specs=[pl.BlockSpec((nseq, 8, KV_DIM), r3), pl.BlockSpec((nseq, conv_ch), r2),
                   pl.BlockSpec((nseq, wb, KV_DIM), r3), pl.BlockSpec((nseq, wb, KV_DIM), r3),
                   pl.BlockSpec((nseq, hist, conv_ch), r3)],
        out_shape=[jax.ShapeDtypeStruct((n, 8, KV_DIM), F32), jax.ShapeDtypeStruct((n, conv_ch), F32),
                   jax.ShapeDtypeStruct((n, wb, KV_DIM), F32), jax.ShapeDtypeStruct((n, wb, KV_DIM), F32),
                   jax.ShapeDtypeStruct((n, hist, conv_ch), F32)],
        compiler_params=pltpu.CompilerParams(dimension_semantics=("arbitrary",),
                                             vmem_limit_bytes=VMEM_LIMIT_BYTES),
        name="mix_sample",
    )(q_rep, k, v, u, cache_k, cache_v, cache_conv, sink_rows, conv_w, conv_b, ln_g, ln_b)


def _out_mlp_kernel(x_ref, a_ref, c_ref, wo_ref, bo_ref, g2_ref, wu_ref, wd_ref, gf_ref,
                    o_ref, hn_ref, *, final_norm):
    f = pl.program_id(1)

    @pl.when(f == 0)
    def _():
        mix = _dot(a_ref[...], wo_ref[0:Q_DIM, :]) + _dot(c_ref[...], wo_ref[Q_DIM:, :])
        h = x_ref[...] + mix + bo_ref[...]
        o_ref[...] = h
        hn_ref[...] = _rms_norm_f32(h, g2_ref[...]).astype(hn_ref.dtype)

    up = _dot(hn_ref[...], wu_ref[...])
    act = jnp.square(jnp.maximum(up, 0.0)).astype(BF16)
    o_ref[...] += _dot(act, wd_ref[...])

    if final_norm:
        @pl.when(f == pl.num_programs(1) - 1)
        def _():
            o_ref[...] = _rms_norm_f32(o_ref[...], gf_ref[...])


def _out_mlp(x, a, c, w_out, b_out, g2, w_up, w_down, gf, tm, tf, final_norm):
    m, d = x.shape
    mixw = w_out.shape[0]
    conv_ch = mixw - Q_DIM
    ff = w_up.shape[1]
    row = lambda i, f: (i, 0)
    return pl.pallas_call(
        functools.partial(_out_mlp_kernel, final_norm=final_norm),
        grid=(m // tm, ff // tf),
        in_specs=[pl.BlockSpec((tm, d), row), pl.BlockSpec((tm, Q_DIM), row),
                  pl.BlockSpec((tm, conv_ch), row),
                  _resident((mixw, d)), _resident((1, d)), _resident((1, d)),
                  pl.BlockSpec((d, tf), lambda i, f: (0, f)), pl.BlockSpec((tf, d), lambda i, f: (f, 0)),
                  _resident((1, d))],
        out_specs=pl.BlockSpec((tm, d), row),
        out_shape=jax.ShapeDtypeStruct((m, d), F32),
        scratch_shapes=[pltpu.VMEM((tm, d), BF16)],
        compiler_params=pltpu.CompilerParams(dimension_semantics=("arbitrary", "arbitrary"),
                                             vmem_limit_bytes=VMEM_LIMIT_BYTES),
        name="out_mlp",
    )(x, a, c, w_out, b_out, g2, w_up, w_down, gf)


def _rope_tables(pos):
    half = ROT_DIM // 2
    inv_freq = jnp.power(jnp.float32(ROPE_THETA), -jnp.arange(half, dtype=jnp.float32) * 2.0 / ROT_DIM)
    ang = pos.astype(jnp.float32)[:, None] * inv_freq[None, :]
    cos, sin = jnp.cos(ang), jnp.sin(ang)
    t = pos.shape[0]
    pad = jnp.zeros((t, HEAD_DIM - ROT_DIM), F32)
    zeros = jnp.zeros((t, half), F32)
    per_head = lambda a, b, fill: jnp.concatenate([a, b, pad + fill], axis=1)
    reps = LANES // HEAD_DIM
    cos_t = jnp.tile(per_head(cos, cos, 1.0), (1, reps))
    sup_t = jnp.tile(per_head(zeros, sin, 0.0), (1, reps))
    sdn_t = jnp.tile(per_head(-sin, zeros, 0.0), (1, reps))
    return cos_t, sup_t, sdn_t


def _query_column_order():
    c = np.arange(Q_DIM)
    i, g, d = c // KV_DIM, (c // HEAD_DIM) % N_KV_HEADS, c % HEAD_DIM
    return (g * GROUP + i) * HEAD_DIM + d


def _tile(m, target):
    t = min(m, target)
    assert m % t == 0, (m, t)
    return t


def kernel(x_prompt, x_sample, cache_k, cache_v, cache_conv, norm1_g, w_in, b_in, attn_sinks,
           conv_w, conv_b, conv_ln_g, conv_ln_b, w_out, b_out, norm2_g, w_up, w_down, final_norm_g):
    batch, seq, d_model = x_prompt.shape
    nsamp, t_s, _ = x_sample.shape
    depth = w_in.shape[0]
    conv_ch = conv_w.shape[2]
    wb = cache_k.shape[2]
    assert t_s == 1 and seq % WINDOW == 0 and wb == WINDOW and cache_conv.shape[2] == CONV_W - 1

    qcols = _query_column_order()
    in_cols = np.concatenate([qcols, np.arange(Q_DIM, w_in.shape[2])])
    out_rows = np.concatenate([qcols, np.arange(Q_DIM, w_out.shape[1])])
    tabs_p = _rope_tables(jnp.arange(seq, dtype=jnp.int32))
    pos_s = PAST_LEN + jnp.arange(t_s, dtype=jnp.int32)
    tabs_s = tuple(jnp.broadcast_to(t, (nsamp, LANES)) for t in _rope_tables(pos_s))
    row = lambda a: a.reshape(1, -1)

    hp = x_prompt.reshape(batch * seq, d_model)
    hs = x_sample.reshape(nsamp * t_s, d_model)
    tm_in = _tile(seq, 512)
    tm_mlp = _tile(batch * seq, 512)
    tf = _tile(w_up.shape[2], 512)
    pk, pv, pc, sk, sv, sc = [], [], [], [], [], []
    for l in range(depth):
        last = l == depth - 1
        w_in_l = w_in[l][:, in_cols].astype(BF16)
        b_in_l = row(b_in[l][in_cols])
        w_out_l = w_out[l][out_rows, :].astype(BF16)
        w_up_l, w_down_l = w_up[l].astype(BF16), w_down[l].astype(BF16)
        sinks = attn_sinks[l].astype(F32)
        sink_rows = sinks.reshape(N_KV_HEADS, GROUP).T.reshape(N_HEADS, 1)
        g1, g2, gf = row(norm1_g[l]), row(norm2_g[l]), row(final_norm_g)
        cb, lg, lb = row(conv_b[l]), row(conv_ln_g[l]), row(conv_ln_b[l])

        q, k, v, u = _in_proj(hp, g1, w_in_l, b_in_l, *tabs_p, tm_in, BF16)
        a_o, c_o = _mix_prompt(q, k, v, u, sinks, conv_w[l], cb, lg, lb, batch, seq)
        hp = _out_mlp(hp, a_o, c_o, w_out_l, row(b_out[l]), g2, w_up_l, w_down_l, gf, tm_mlp, tf, last)
        pk.append(k.reshape(batch, seq, N_KV_HEADS, HEAD_DIM)[:, -WINDOW:])
        pv.append(v.reshape(batch, seq, N_KV_HEADS, HEAD_DIM)[:, -WINDOW:])
        pc.append(u.reshape(batch, seq, conv_ch)[:, -(CONV_W - 1):])

        q, k, v, u = _in_proj(hs, g1, w_in_l, b_in_l, *tabs_s, nsamp, F32)
        q_rep = jnp.repeat(q.reshape(nsamp, GROUP, KV_DIM), N_KV_HEADS, axis=1)
        a8, c_o, nk, nv, nc = _mix_sample(
            q_rep, k, v, u,
            cache_k[l].reshape(nsamp, wb, KV_DIM), cache_v[l].reshape(nsamp, wb, KV_DIM), cache_conv[l],
            sink_rows, conv_w[l], cb, lg, lb, _tile(nsamp, 8))
        a_o = a8[:, :GROUP, :].reshape(nsamp, Q_DIM)
        hs = _out_mlp(hs, a_o.astype(BF16), c_o.astype(BF16), w_out_l, row(b_out[l]), g2,
                      w_up_l, w_down_l, gf, nsamp, tf, last)
        sk.append(nk.reshape(nsamp, wb, N_KV_HEADS, HEAD_DIM))
        sv.append(nv.reshape(nsamp, wb, N_KV_HEADS, HEAD_DIM))
        sc.append(nc)

    if depth == 0:
        raise NotImplementedError("depth 0")
    y_prompt = hp.reshape(batch, seq, d_model)
    y_sample = hs.reshape(nsamp, t_s, d_model)
    return (y_prompt, y_sample, jnp.stack(pk), jnp.stack(pv), jnp.stack(pc),
            jnp.stack(sk), jnp.stack(sv), jnp.stack(sc))
```

```python
import functools

import jax
import jax.numpy as jnp
import numpy as np
from jax import lax
from jax.experimental import pallas as pl
from jax.experimental.pallas import tpu as pltpu

HEAD_DIM = 64
N_HEADS = 16
N_KV_HEADS = 4
GROUP = N_HEADS // N_KV_HEADS
WINDOW = 128
ROT_DIM = HEAD_DIM // 4
ROPE_THETA = 500000.0
ATTN_SCALE = HEAD_DIM ** -0.5
Q_DIM = N_HEADS * HEAD_DIM
KV_DIM = N_KV_HEADS * HEAD_DIM
CONV_W = 31
PAST_LEN = 16384
EPS = 1e-5

LANES = 128
SUBLANES = 8
CONV_HALO = 32
VMEM_LIMIT_BYTES = 56 * 1024 * 1024
PARTS = GROUP

BF16 = jnp.bfloat16
F32 = jnp.float32


def _resident(shape):
    return pl.BlockSpec(shape, lambda *_: (0,) * len(shape), pipeline_mode=pl.Buffered(1))


def _params(*semantics):
    return pltpu.CompilerParams(dimension_semantics=semantics, vmem_limit_bytes=VMEM_LIMIT_BYTES)


def _rms_norm_f32(x, g):
    return x * lax.rsqrt(jnp.mean(x * x, axis=-1, keepdims=True) + EPS) * g


def _layer_norm_swish(y, g, b):
    mu = jnp.mean(y, axis=-1, keepdims=True)
    yc = y - mu
    var = jnp.mean(yc * yc, axis=-1, keepdims=True)
    yn = yc * lax.rsqrt(var + EPS) * g + b
    return yn * jax.nn.sigmoid(yn)


def _dot(a, b):
    return jnp.dot(a, b, preferred_element_type=F32)


def _dot_nt(a, b):
    return lax.dot_general(a, b, (((1,), (1,)), ((), ())), preferred_element_type=F32)


def _sink_softmax(s, sink):
    m = jnp.maximum(jnp.max(s, axis=-1, keepdims=True), sink)
    p = jnp.exp(s - m)
    return p / (jnp.sum(p, axis=-1, keepdims=True) + jnp.exp(sink - m))


def _rope(z, cos, sin_up, sin_dn, axis):
    n = z.shape[axis]
    half = ROT_DIM // 2
    return z * cos + pltpu.roll(z, half, axis) * sin_up + pltpu.roll(z, n - half, axis) * sin_dn


def _rope_lanes(z, cos, sin_up, sin_dn):
    cols = [_rope(z[:, c * LANES:(c + 1) * LANES], cos, sin_up, sin_dn, 1) for c in range(z.shape[1] // LANES)]
    return jnp.concatenate(cols, axis=1)


def _in_proj_kernel(x_ref, g_ref, w_ref, b_ref, cos_ref, sup_ref, sdn_ref,
                    q_ref, k_ref, v_ref, u_ref, pk_ref, pv_ref, pc_ref):
    conv_ch = u_ref.shape[1]
    tm = x_ref.shape[0]
    hist = pc_ref.shape[1]
    o1, o2, o3, o4 = Q_DIM, Q_DIM + KV_DIM, Q_DIM + 2 * KV_DIM, Q_DIM + 2 * KV_DIM + conv_ch
    hn = _rms_norm_f32(x_ref[...], g_ref[...]).astype(BF16)
    cos, sup, sdn = cos_ref[...], sup_ref[...], sdn_ref[...]

    zq = _dot(hn, w_ref[:, 0:o1]) + b_ref[:, 0:o1]
    q_ref[...] = (_rope_lanes(zq, cos, sup, sdn) * ATTN_SCALE).astype(q_ref.dtype)
    zk = _dot(hn, w_ref[:, o1:o2]) + b_ref[:, o1:o2]
    k_ref[...] = _rope_lanes(zk, cos, sup, sdn)
    v_ref[...] = _dot(hn, w_ref[:, o2:o3]) + b_ref[:, o2:o3]
    zu = _dot(hn, w_ref[:, o3:o4]) + b_ref[:, o3:o4]
    zg = _dot(hn, w_ref[:, o4:]) + b_ref[:, o4:]
    u_ref[...] = zu * jax.nn.sigmoid(zg)
    pk_ref[0] = k_ref[tm - WINDOW:, :]
    pv_ref[0] = v_ref[tm - WINDOW:, :]
    pc_ref[0] = u_ref[tm - hist:, :]


def _in_proj(x, g, w, b, cos, sup, sdn, tm, batch):
    m, d = x.shape
    n = w.shape[1]
    conv_ch = (n - Q_DIM - 2 * KV_DIM) // 2
    nt = cos.shape[0] // tm
    hist = CONV_W - 1
    row = lambda i: (i, 0)
    tab = lambda i: (i % nt, 0)
    seq = lambda i: (i // nt, 0, 0)
    return pl.pallas_call(
        _in_proj_kernel,
        grid=(m // tm,),
        in_specs=[pl.BlockSpec((tm, d), row), _resident((1, d)), _resident((d, n)), _resident((1, n)),
                  pl.BlockSpec((tm, LANES), tab), pl.BlockSpec((tm, LANES), tab),
                  pl.BlockSpec((tm, LANES), tab)],
        out_specs=[pl.BlockSpec((tm, Q_DIM), row), pl.BlockSpec((tm, KV_DIM), row),
                   pl.BlockSpec((tm, KV_DIM), row), pl.BlockSpec((tm, conv_ch), row),
                   pl.BlockSpec((1, WINDOW, KV_DIM), seq), pl.BlockSpec((1, WINDOW, KV_DIM), seq),
                   pl.BlockSpec((1, hist, conv_ch), seq)],
        out_shape=[jax.ShapeDtypeStruct((m, Q_DIM), BF16), jax.ShapeDtypeStruct((m, KV_DIM), F32),
                   jax.ShapeDtypeStruct((m, KV_DIM), F32), jax.ShapeDtypeStruct((m, conv_ch), F32),
                   jax.ShapeDtypeStruct((batch, WINDOW, KV_DIM), F32),
                   jax.ShapeDtypeStruct((batch, WINDOW, KV_DIM), F32),
                   jax.ShapeDtypeStruct((batch, hist, conv_ch), F32)],
        compiler_params=_params("arbitrary"),
        name="in_proj",
    )(x, g, w, b, cos, sup, sdn)


def _attn_part(q_ref, k_ref, kp_ref, v_ref, vp_ref, sink_ref, a_out, part, has_prev):
    blk = q_ref.shape[0]
    kk = jnp.concatenate([kp_ref[...], k_ref[...]], axis=0).astype(BF16)
    vv = jnp.concatenate([vp_ref[...], v_ref[...]], axis=0).astype(BF16)
    lane_group = lax.broadcasted_iota(jnp.int32, kk.shape, 1) // HEAD_DIM
    zero = jnp.zeros_like(kk)
    kstack = jnp.concatenate([jnp.where(lane_group == g, kk, zero) for g in range(N_KV_HEADS)], axis=0)
    vstack = jnp.concatenate([jnp.where(lane_group == g, vv, zero) for g in range(N_KV_HEADS)], axis=0)

    qi = lax.broadcasted_iota(jnp.int32, (blk, 2 * blk), 0)
    kj = lax.broadcasted_iota(jnp.int32, (blk, 2 * blk), 1)
    rel = qi + blk - kj
    valid = (rel >= 0) & (rel < WINDOW) & (has_prev | (kj >= blk))

    s_all = _dot_nt(q_ref[...], kstack)
    probs = []
    for g in range(N_KV_HEADS):
        s = jnp.where(valid, s_all[:, g * 2 * blk:(g + 1) * 2 * blk], -jnp.inf)
        probs.append(_sink_softmax(s, sink_ref[g * GROUP + part]).astype(BF16))
    a_out[...] = _dot(jnp.concatenate(probs, axis=1), vstack).astype(a_out.dtype)


def _conv_part(u_ref, uh_ref, cw_ref, cb_ref, ue_ref, c_out, has_prev):
    blk = u_ref.shape[0]
    halo = uh_ref[...]
    ue_ref[0:CONV_HALO, :] = jnp.where(has_prev, halo, jnp.zeros_like(halo))
    ue_ref[CONV_HALO:, :] = u_ref[...]
    first = CONV_HALO - (CONV_W - 1)
    for c in range(u_ref.shape[1] // LANES):
        cs = slice(c * LANES, (c + 1) * LANES)
        out = cb_ref[:, cs]
        for r in range(SUBLANES):
            rows = blk + (SUBLANES if r else 0)
            part = None
            for a in range((first + CONV_W - 1 - r) // SUBLANES + 1):
                t = a * SUBLANES + r - first
                if 0 <= t < CONV_W:
                    term = cw_ref[t:t + 1, cs] * ue_ref[a * SUBLANES:a * SUBLANES + rows, cs]
                    part = term if part is None else part + term
            out = out + part[r:r + blk, :]
        c_out[:, cs] = out


def _mlp_rows(hn_ref, wu_ref, wd_ref, o_ref, rows):
    up = _dot(hn_ref[rows, :], wu_ref[...])
    act = jnp.square(jnp.maximum(up, 0.0)).astype(BF16)
    o_ref[rows, :] += _dot(act, wd_ref[...])


def _mlp_mix_kernel(x_ref, q_ref, k_ref, kp_ref, v_ref, vp_ref, u_ref, uh_ref, sink_ref, cw_ref, cb_ref,
                    lg_ref, lb_ref, wo_ref, bo_ref, g2_ref, gf_ref, wu_ref, wd_ref,
                    o_ref, hn_ref, a_s, c_s, ue_ref, *, nb, nrow, final_norm):
    i, f = pl.program_id(0), pl.program_id(1)
    blk = q_ref.shape[0]
    blocks_per_tile = x_ref.shape[0] // blk
    slot_mix, slot_mlp = i % 2, (i + 1) % 2

    @pl.when((i == 0) & (f == 0))
    def _():
        a_s[...] = jnp.zeros_like(a_s)
        c_s[...] = jnp.zeros_like(c_s)

    @pl.when(f == 0)
    def _():
        a = jnp.concatenate([a_s[slot_mlp, p] for p in range(PARTS)], axis=1)
        conv = jnp.concatenate([c_s[slot_mlp, p] for p in range(PARTS)], axis=1)
        c = _layer_norm_swish(conv, lg_ref[...], lb_ref[...]).astype(BF16)
        mix = _dot(a, wo_ref[0:Q_DIM, :]) + _dot(c, wo_ref[Q_DIM:, :])
        h = x_ref[...] + mix + bo_ref[...]
        o_ref[...] = h
        hn_ref[...] = _rms_norm_f32(h, g2_ref[...]).astype(hn_ref.dtype)

    sub, part = f // PARTS, f % PARTS
    gblk = jnp.minimum(i, nrow - 1) * blocks_per_tile + sub
    has_prev = gblk % nb != 0
    rows = pl.ds(pl.multiple_of(sub * blk, blk), blk)
    _attn_part(q_ref, k_ref, kp_ref, v_ref, vp_ref, sink_ref, a_s.at[slot_mix, part, rows, :], part, has_prev)
    _conv_part(u_ref, uh_ref, cw_ref, cb_ref, ue_ref, c_s.at[slot_mix, part, rows, :], has_prev)
    _mlp_rows(hn_ref, wu_ref, wd_ref, o_ref, slice(None))

    if final_norm:
        @pl.when(f == pl.num_programs(1) - 1)
        def _():
            o_ref[...] = _rms_norm_f32(o_ref[...], gf_ref[...])


def _mlp_mix(x, q, k, v, u, sinks, conv_w, conv_b, ln_g, ln_b, w_out, b_out, g2, gf, w_up, w_down,
             seq, tm, final_norm):
    m, d = x.shape
    conv_ch = u.shape[1]
    ff = w_up.shape[1]
    blk = WINDOW
    nb = seq // blk
    nrow = m // tm
    bpt = tm // blk
    nf = bpt * PARTS
    assert ff % nf == 0 and conv_ch == PARTS * KV_DIM and Q_DIM == PARTS * KV_DIM
    tf = ff // nf
    hpb = blk // CONV_HALO
    gblk = lambda i, f: jnp.minimum(i, nrow - 1) * bpt + f // PARTS
    mlp_tile = lambda i, f: (jnp.maximum(i - 1, 0), 0)
    own = lambda i, f: (gblk(i, f), 0)
    prev = lambda i, f: (jnp.maximum(gblk(i, f) - 1, 0), 0)
    own_part = lambda i, f: (gblk(i, f), f % PARTS)
    halo_part = lambda i, f: (jnp.maximum(gblk(i, f) * hpb - 1, 0), f % PARTS)
    col_part = lambda i, f: (0, f % PARTS)
    return pl.pallas_call(
        functools.partial(_mlp_mix_kernel, nb=nb, nrow=nrow, final_norm=final_norm),
        grid=(nrow + 1, nf),
        in_specs=[pl.BlockSpec((tm, d), mlp_tile),
                  pl.BlockSpec((blk, KV_DIM), own_part),
                  pl.BlockSpec((blk, KV_DIM), own), pl.BlockSpec((blk, KV_DIM), prev),
                  pl.BlockSpec((blk, KV_DIM), own), pl.BlockSpec((blk, KV_DIM), prev),
                  pl.BlockSpec((blk, KV_DIM), own_part), pl.BlockSpec((CONV_HALO, KV_DIM), halo_part),
                  pl.BlockSpec(memory_space=pltpu.SMEM),
                  pl.BlockSpec((CONV_W, KV_DIM), col_part), pl.BlockSpec((1, KV_DIM), col_part),
                  _resident((1, conv_ch)), _resident((1, conv_ch)),
                  _resident((Q_DIM + conv_ch, d)), _resident((1, d)), _resident((1, d)), _resident((1, d)),
                  pl.BlockSpec((d, tf), lambda i, f: (0, f)), pl.BlockSpec((tf, d), lambda i, f: (f, 0))],
        out_specs=pl.BlockSpec((tm, d), mlp_tile),
        out_shape=jax.ShapeDtypeStruct((m, d), F32),
        scratch_shapes=[pltpu.VMEM((tm, d), BF16),
                        pltpu.VMEM((2, PARTS, tm, KV_DIM), BF16),
                        pltpu.VMEM((2, PARTS, tm, KV_DIM), F32),
                        pltpu.VMEM((CONV_HALO + blk, KV_DIM), F32)],
        compiler_params=_params("arbitrary", "arbitrary"),
        name="mlp_mix",
    )(x, q, k, k, v, v, u, u, sinks, conv_w, conv_b, ln_g, ln_b, w_out, b_out, g2, gf, w_up, w_down)


def _in_proj_sample_kernel(x_ref, g_ref, wq_ref, bq_ref, wkv_ref, bkv_ref, wu_ref, bu_ref,
                           cos_ref, sup_ref, sdn_ref, cosc_ref, supc_ref, sdnc_ref,
                           q_ref, kt_ref, vt_ref, u_ref):
    conv_ch = u_ref.shape[1]
    hn = _rms_norm_f32(x_ref[...], g_ref[...]).astype(BF16)
    zq = _dot(hn, wq_ref[...]) + bq_ref[...]
    q_ref[...] = _rope_lanes(zq, cos_ref[...], sup_ref[...], sdn_ref[...]) * ATTN_SCALE
    zkv = _dot_nt(wkv_ref[...], hn) + bkv_ref[...]
    kt_ref[...] = _rope(zkv[0:KV_DIM, :], cosc_ref[...], supc_ref[...], sdnc_ref[...], 0)
    vt_ref[...] = zkv[KV_DIM:, :]
    zu = _dot(hn, wu_ref[:, 0:conv_ch]) + bu_ref[:, 0:conv_ch]
    zg = _dot(hn, wu_ref[:, conv_ch:]) + bu_ref[:, conv_ch:]
    u_ref[...] = zu * jax.nn.sigmoid(zg)


def _in_proj_sample(x, g, wq, bq, wkv_t, bkv_c, wu, bu, row_tabs, col_tabs):
    n, d = x.shape
    conv_ch = wu.shape[1] // 2
    full = lambda a: _resident(a.shape)
    args = (x, g, wq, bq, wkv_t, bkv_c, wu, bu, *row_tabs, *col_tabs)
    return pl.pallas_call(
        _in_proj_sample_kernel,
        grid=(1,),
        in_specs=[full(a) for a in args],
        out_specs=[_resident((n, Q_DIM)), _resident((KV_DIM, n)), _resident((KV_DIM, n)), _resident((n, conv_ch))],
        out_shape=[jax.ShapeDtypeStruct((n, Q_DIM), F32), jax.ShapeDtypeStruct((KV_DIM, n), F32),
                   jax.ShapeDtypeStruct((KV_DIM, n), F32), jax.ShapeDtypeStruct((n, conv_ch), F32)],
        compiler_params=_params("arbitrary"),
        name="in_proj_sample",
    )(*args)


def _attn_sample_kernel(q_ref, kn_ref, vn_ref, ck_ref, cv_ref, sink_ref, ao_ref, nk_ref, nv_ref):
    nseq, _, wb = ck_ref.shape
    rows = N_HEADS
    lane_group = lax.broadcasted_iota(jnp.int32, (rows, KV_DIM), 1) // HEAD_DIM
    row_group = lax.broadcasted_iota(jnp.int32, (rows, KV_DIM), 0) % N_KV_HEADS
    own_lanes = lane_group == row_group
    fold = (lax.broadcasted_iota(jnp.int32, (SUBLANES, rows), 1) // N_KV_HEADS
            == lax.broadcasted_iota(jnp.int32, (SUBLANES, rows), 0)).astype(F32)
    newest = lax.broadcasted_iota(jnp.int32, (KV_DIM, wb), 1) == wb - 1
    sink = sink_ref[...]
    kn, vn = kn_ref[0], vn_ref[0]

    for n in range(nseq):
        knew = jnp.where(newest, kn[:, n:n + 1], pltpu.roll(ck_ref[n], wb - 1, 1))
        vnew = jnp.where(newest, vn[:, n:n + 1], pltpu.roll(cv_ref[n], wb - 1, 1))
        nk_ref[n] = knew
        nv_ref[n] = vnew
        qrows = jnp.where(own_lanes, q_ref[n], jnp.zeros((rows, KV_DIM), F32))
        s = _dot(qrows.astype(BF16), knew.astype(BF16))
        o = _dot_nt(_sink_softmax(s, sink).astype(BF16), vnew.astype(BF16))
        o = jnp.where(own_lanes, o, jnp.zeros_like(o))
        ao_ref[n] = _dot(fold, o)


def _attn_sample(q_rep, kn, vn, cache_kt, cache_vt, sink_rows, nseq):
    n, _, wb = cache_kt.shape
    r3 = lambda s: (s, 0, 0)
    return pl.pallas_call(
        _attn_sample_kernel,
        grid=(n // nseq,),
        in_specs=[pl.BlockSpec((nseq, N_HEADS, KV_DIM), r3),
                  pl.BlockSpec((1, KV_DIM, nseq), r3), pl.BlockSpec((1, KV_DIM, nseq), r3),
                  pl.BlockSpec((nseq, KV_DIM, wb), r3), pl.BlockSpec((nseq, KV_DIM, wb), r3),
                  _resident((N_HEADS, 1))],
        out_specs=[pl.BlockSpec((nseq, SUBLANES, KV_DIM), r3),
                   pl.BlockSpec((nseq, KV_DIM, wb), r3), pl.BlockSpec((nseq, KV_DIM, wb), r3)],
        out_shape=[jax.ShapeDtypeStruct((n, SUBLANES, KV_DIM), F32),
                   jax.ShapeDtypeStruct((n, KV_DIM, wb), F32), jax.ShapeDtypeStruct((n, KV_DIM, wb), F32)],
        compiler_params=_params("arbitrary"),
        name="attn_sample",
    )(q_rep, kn, vn, cache_kt, cache_vt, sink_rows)


def _conv_sample_kernel(cc_ref, cn_ref, u_ref, cw_ref, cb_ref, lg_ref, lb_ref, co_ref, nc_ref, acc_ref, *, hist):
    j = pl.program_id(0)
    u = u_ref[...]

    @pl.when(j == 0)
    def _():
        acc_ref[...] = jnp.zeros_like(acc_ref)

    acc_ref[...] += cw_ref[j] * cc_ref[0]

    @pl.when(j < hist - 1)
    def _():
        nc_ref[0] = cn_ref[0]

    @pl.when(j == hist - 1)
    def _():
        nc_ref[0] = u
        y = acc_ref[...] + cw_ref[hist] * u + cb_ref[...]
        co_ref[...] = _layer_norm_swish(y, lg_ref[...], lb_ref[...]).astype(co_ref.dtype)


def _conv_sample(cache_t, u, conv_w, conv_b, ln_g, ln_b):
    hist, n, ch = cache_t.shape
    tap = lambda j: (j, 0, 0)
    return pl.pallas_call(
        functools.partial(_conv_sample_kernel, hist=hist),
        grid=(hist,),
        in_specs=[pl.BlockSpec((1, n, ch), tap),
                  pl.BlockSpec((1, n, ch), lambda j: (jnp.minimum(j + 1, hist - 1), 0, 0)),
                  _resident((n, ch)), _resident((hist + 1, 1, ch)),
                  _resident((1, ch)), _resident((1, ch)), _resident((1, ch))],
        out_specs=[_resident((n, ch)), pl.BlockSpec((1, n, ch), tap)],
        out_shape=[jax.ShapeDtypeStruct((n, ch), BF16), jax.ShapeDtypeStruct((hist, n, ch), F32)],
        scratch_shapes=[pltpu.VMEM((n, ch), F32)],
        compiler_params=_params("arbitrary"),
        name="conv_sample",
    )(cache_t, cache_t, u, conv_w.reshape(hist + 1, 1, ch), conv_b, ln_g, ln_b)


def _out_mlp_kernel(x_ref, a_ref, c_ref, wo_ref, bo_ref, g2_ref, wu_ref, wd_ref, gf_ref,
                    o_ref, hn_ref, *, final_norm):
    f = pl.program_id(1)

    @pl.when(f == 0)
    def _():
        mix = _dot(a_ref[...], wo_ref[0:Q_DIM, :]) + _dot(c_ref[...], wo_ref[Q_DIM:, :])
        h = x_ref[...] + mix + bo_ref[...]
        o_ref[...] = h
        hn_ref[...] = _rms_norm_f32(h, g2_ref[...]).astype(hn_ref.dtype)

    up = _dot(hn_ref[...], wu_ref[...])
    act = jnp.square(jnp.maximum(up, 0.0)).astype(BF16)
    o_ref[...] += _dot(act, wd_ref[...])

    if final_norm:
        @pl.when(f == pl.num_programs(1) - 1)
        def _():
            o_ref[...] = _rms_norm_f32(o_ref[...], gf_ref[...])


def _out_mlp(x, a, c, w_out, b_out, g2, w_up, w_down, gf, tm, tf, final_norm):
    m, d = x.shape
    mixw = w_out.shape[0]
    conv_ch = mixw - Q_DIM
    ff = w_up.shape[1]
    row = lambda i, f: (i, 0)
    return pl.pallas_call(
        functools.partial(_out_mlp_kernel, final_norm=final_norm),
        grid=(m // tm, ff // tf),
        in_specs=[pl.BlockSpec((tm, d), row), pl.BlockSpec((tm, Q_DIM), row),
                  pl.BlockSpec((tm, conv_ch), row),
                  _resident((mixw, d)), _resident((1, d)), _resident((1, d)),
                  pl.BlockSpec((d, tf), lambda i, f: (0, f)), pl.BlockSpec((tf, d), lambda i, f: (f, 0)),
                  _resident((1, d))],
        out_specs=pl.BlockSpec((tm, d), row),
        out_shape=jax.ShapeDtypeStruct((m, d), F32),
        scratch_shapes=[pltpu.VMEM((tm, d), BF16)],
        compiler_params=_params("arbitrary", "arbitrary"),
        name="out_mlp",
    )(x, a, c, w_out, b_out, g2, w_up, w_down, gf)


def _rope_tables(pos):
    half = ROT_DIM // 2
    inv_freq = jnp.power(jnp.float32(ROPE_THETA), -jnp.arange(half, dtype=jnp.float32) * 2.0 / ROT_DIM)
    ang = pos.astype(jnp.float32)[:, None] * inv_freq[None, :]
    cos, sin = jnp.cos(ang), jnp.sin(ang)
    t = pos.shape[0]
    pad = jnp.zeros((t, HEAD_DIM - ROT_DIM), F32)
    zeros = jnp.zeros((t, half), F32)
    per_head = lambda a, b, fill: jnp.concatenate([a, b, pad + fill], axis=1)
    reps = LANES // HEAD_DIM
    cos_t = jnp.tile(per_head(cos, cos, 1.0), (1, reps))
    sup_t = jnp.tile(per_head(zeros, sin, 0.0), (1, reps))
    sdn_t = jnp.tile(per_head(-sin, zeros, 0.0), (1, reps))
    return cos_t, sup_t, sdn_t


def _heads_to_igd(a, axis):
    shape = a.shape
    a = a.reshape(shape[:axis] + (N_KV_HEADS, GROUP, HEAD_DIM) + shape[axis + 1:])
    return jnp.swapaxes(a, axis, axis + 1).reshape(shape)


def _tile(m, target):
    t = min(m, target)
    assert m % t == 0, (m, t)
    return t


def kernel(x_prompt, x_sample, cache_k, cache_v, cache_conv, norm1_g, w_in, b_in, attn_sinks,
           conv_w, conv_b, conv_ln_g, conv_ln_b, w_out, b_out, norm2_g, w_up, w_down, final_norm_g):
    batch, seq, d_model = x_prompt.shape
    nsamp, t_s, _ = x_sample.shape
    depth = w_in.shape[0]
    conv_ch = conv_w.shape[2]
    wb = cache_k.shape[2]
    hist = CONV_W - 1
    assert t_s == 1 and seq % WINDOW == 0 and wb == WINDOW and cache_conv.shape[2] == hist
    assert depth >= 1

    tabs_p = _rope_tables(jnp.arange(seq, dtype=jnp.int32))
    tabs_s1 = _rope_tables(PAST_LEN + jnp.arange(t_s, dtype=jnp.int32))
    tabs_s = tuple(jnp.broadcast_to(t, (nsamp, LANES)) for t in tabs_s1)
    tabs_sc = tuple(jnp.tile(t[0], KV_DIM // LANES).reshape(KV_DIM, 1) for t in tabs_s1)
    row = lambda a: a.reshape(1, -1)

    hp = x_prompt.reshape(batch * seq, d_model)
    hs = x_sample.reshape(nsamp * t_s, d_model)
    tm_in = _tile(seq, 512)
    tm_mlp = _tile(batch * seq, 512)
    tf_s = _tile(w_up.shape[2], 1024)
    nseq = _tile(nsamp, 8)
    o1, o2 = Q_DIM, Q_DIM + 2 * KV_DIM
    pk, pv, pc, sk, sv, sc = [], [], [], [], [], []
    for l in range(depth):
        last = l == depth - 1
        wq = _heads_to_igd(w_in[l][:, :o1], 1).astype(BF16)
        bq = _heads_to_igd(b_in[l][:o1], 0)
        w_rest = w_in[l][:, o1:].astype(BF16)
        w_in_l = jnp.concatenate([wq, w_rest], axis=1)
        b_in_l = row(jnp.concatenate([bq, b_in[l][o1:]]))
        w_out_l = jnp.concatenate([_heads_to_igd(w_out[l][:Q_DIM], 0), w_out[l][Q_DIM:]], axis=0).astype(BF16)
        w_up_l, w_down_l = w_up[l].astype(BF16), w_down[l].astype(BF16)
        sinks = attn_sinks[l].astype(F32)
        sink_rows = sinks.reshape(N_KV_HEADS, GROUP).T.reshape(N_HEADS, 1)
        g1, g2, gf = row(norm1_g[l]), row(norm2_g[l]), row(final_norm_g)
        cb, lg, lb, bo = row(conv_b[l]), row(conv_ln_g[l]), row(conv_ln_b[l]), row(b_out[l])

        q, k, v, u, pk_l, pv_l, pc_l = _in_proj(hp, g1, w_in_l, b_in_l, *tabs_p, tm_in, batch)
        hp = _mlp_mix(hp, q, k, v, u, sinks, conv_w[l], cb, lg, lb, w_out_l, bo, g2, gf, w_up_l, w_down_l,
                      seq, tm_mlp, last)
        pk.append(pk_l.reshape(batch, WINDOW, N_KV_HEADS, HEAD_DIM))
        pv.append(pv_l.reshape(batch, WINDOW, N_KV_HEADS, HEAD_DIM))
        pc.append(pc_l)

        cache_kt = jnp.transpose(cache_k[l], (0, 2, 3, 1)).reshape(nsamp, KV_DIM, wb)
        cache_vt = jnp.transpose(cache_v[l], (0, 2, 3, 1)).reshape(nsamp, KV_DIM, wb)
        cache_ct = jnp.transpose(cache_conv[l], (1, 0, 2))
        wkv_t = w_in[l][:, o1:o2].T.astype(BF16)
        bkv_c = b_in[l][o1:o2].reshape(2 * KV_DIM, 1)
        q, kt, vt, u = _in_proj_sample(hs, g1, wq, row(bq), wkv_t, bkv_c, w_rest[:, 2 * KV_DIM:],
                                       row(b_in[l][o2:]), tabs_s, tabs_sc)
        q_rep = jnp.repeat(q.reshape(nsamp, GROUP, KV_DIM), N_KV_HEADS, axis=1)
        by_block = lambda a: a.reshape(KV_DIM, nsamp // nseq, nseq).transpose(1, 0, 2)
        a8, nkt, nvt = _attn_sample(q_rep, by_block(kt), by_block(vt), cache_kt, cache_vt, sink_rows, nseq)
        c_o, nct = _conv_sample(cache_ct, u, conv_w[l], cb, lg, lb)
        a_o = a8[:, :GROUP, :].reshape(nsamp, Q_DIM).astype(BF16)
        hs = _out_mlp(hs, a_o, c_o, w_out_l, bo, g2, w_up_l, w_down_l, gf, nsamp, tf_s, last)
        to_cache = lambda a: jnp.transpose(a.reshape(nsamp, N_KV_HEADS, HEAD_DIM, wb), (0, 3, 1, 2))
        sk.append(to_cache(nkt))
        sv.append(to_cache(nvt))
        sc.append(jnp.transpose(nct, (1, 0, 2)))

    y_prompt = hp.reshape(batch, seq, d_model)
    y_sample = hs.reshape(nsamp, t_s, d_model)
    return (y_prompt, y_sample, jnp.stack(pk), jnp.stack(pv), jnp.stack(pc),
            jnp.stack(sk), jnp.stack(sv), jnp.stack(sc))
```

```python
import functools

import jax
import jax.numpy as jnp
import numpy as np
from jax import lax
from jax.experimental import pallas as pl
from jax.experimental.pallas import tpu as pltpu

HEAD_DIM = 64
N_HEADS = 16
N_KV_HEADS = 4
GROUP = N_HEADS // N_KV_HEADS
WINDOW = 128
ROT_DIM = HEAD_DIM // 4
ROPE_THETA = 500000.0
ATTN_SCALE = HEAD_DIM ** -0.5
Q_DIM = N_HEADS * HEAD_DIM
KV_DIM = N_KV_HEADS * HEAD_DIM
CONV_W = 31
PAST_LEN = 16384
EPS = 1e-5

LANES = 128
SUBLANES = 8
CONV_HALO = 32
VMEM_LIMIT_BYTES = 56 * 1024 * 1024
PARTS = GROUP

BF16 = jnp.bfloat16
F32 = jnp.float32


def _resident(shape):
    return pl.BlockSpec(shape, lambda *_: (0,) * len(shape), pipeline_mode=pl.Buffered(1))


def _params(*semantics):
    return pltpu.CompilerParams(dimension_semantics=semantics, vmem_limit_bytes=VMEM_LIMIT_BYTES)


def _rms_norm_f32(x, g):
    return x * lax.rsqrt(jnp.mean(x * x, axis=-1, keepdims=True) + EPS) * g


def _layer_norm_swish(y, g, b):
    mu = jnp.mean(y, axis=-1, keepdims=True)
    yc = y - mu
    var = jnp.mean(yc * yc, axis=-1, keepdims=True)
    yn = yc * lax.rsqrt(var + EPS) * g + b
    return yn * jax.nn.sigmoid(yn)


def _dot(a, b):
    return jnp.dot(a, b, preferred_element_type=F32)


def _dot_nt(a, b):
    return lax.dot_general(a, b, (((1,), (1,)), ((), ())), preferred_element_type=F32)


def _sink_softmax(s, sink):
    m = jnp.maximum(jnp.max(s, axis=-1, keepdims=True), sink)
    p = jnp.exp(s - m)
    return p / (jnp.sum(p, axis=-1, keepdims=True) + jnp.exp(sink - m))


def _rope(z, cos, sin_up, sin_dn, axis):
    n = z.shape[axis]
    half = ROT_DIM // 2
    return z * cos + pltpu.roll(z, half, axis) * sin_up + pltpu.roll(z, n - half, axis) * sin_dn


def _rope_lanes(z, cos, sin_up, sin_dn):
    cols = [_rope(z[:, c * LANES:(c + 1) * LANES], cos, sin_up, sin_dn, 1) for c in range(z.shape[1] // LANES)]
    return jnp.concatenate(cols, axis=1)


def _in_proj_kernel(x_ref, g_ref, w_ref, b_ref, cos_ref, sup_ref, sdn_ref,
                    q_ref, k_ref, v_ref, u_ref, pk_ref, pv_ref, pc_ref):
    conv_ch = u_ref.shape[0] * u_ref.shape[2]
    tm = x_ref.shape[0]
    hist = pc_ref.shape[1]
    o1, o2, o3, o4 = Q_DIM, Q_DIM + KV_DIM, Q_DIM + 2 * KV_DIM, Q_DIM + 2 * KV_DIM + conv_ch
    hn = _rms_norm_f32(x_ref[...], g_ref[...]).astype(BF16)
    cos, sup, sdn = cos_ref[...], sup_ref[...], sdn_ref[...]

    zq = _dot(hn, w_ref[:, 0:o1]) + b_ref[:, 0:o1]
    q = (_rope_lanes(zq, cos, sup, sdn) * ATTN_SCALE).astype(q_ref.dtype)
    zk = _dot(hn, w_ref[:, o1:o2]) + b_ref[:, o1:o2]
    k_ref[...] = _rope_lanes(zk, cos, sup, sdn)
    v_ref[...] = _dot(hn, w_ref[:, o2:o3]) + b_ref[:, o2:o3]
    zu = _dot(hn, w_ref[:, o3:o4]) + b_ref[:, o3:o4]
    zg = _dot(hn, w_ref[:, o4:]) + b_ref[:, o4:]
    u = zu * jax.nn.sigmoid(zg)
    for p in range(PARTS):
        q_ref[p] = q[:, p * KV_DIM:(p + 1) * KV_DIM]
        u_ref[p] = u[:, p * KV_DIM:(p + 1) * KV_DIM]
    pk_ref[0] = k_ref[tm - WINDOW:, :]
    pv_ref[0] = v_ref[tm - WINDOW:, :]
    pc_ref[0] = u[tm - hist:, :]


def _in_proj(x, g, w, b, cos, sup, sdn, tm, batch):
    m, d = x.shape
    n = w.shape[1]
    conv_ch = (n - Q_DIM - 2 * KV_DIM) // 2
    nt = cos.shape[0] // tm
    hist = CONV_W - 1
    row = lambda i: (i, 0)
    parts = lambda i: (0, i, 0)
    tab = lambda i: (i % nt, 0)
    seq = lambda i: (i // nt, 0, 0)
    assert conv_ch == PARTS * KV_DIM and Q_DIM == PARTS * KV_DIM
    return pl.pallas_call(
        _in_proj_kernel,
        grid=(m // tm,),
        in_specs=[pl.BlockSpec((tm, d), row), _resident((1, d)), _resident((d, n)), _resident((1, n)),
                  pl.BlockSpec((tm, LANES), tab), pl.BlockSpec((tm, LANES), tab),
                  pl.BlockSpec((tm, LANES), tab)],
        out_specs=[pl.BlockSpec((PARTS, tm, KV_DIM), parts), pl.BlockSpec((tm, KV_DIM), row),
                   pl.BlockSpec((tm, KV_DIM), row), pl.BlockSpec((PARTS, tm, KV_DIM), parts),
                   pl.BlockSpec((1, WINDOW, KV_DIM), seq), pl.BlockSpec((1, WINDOW, KV_DIM), seq),
                   pl.BlockSpec((1, hist, conv_ch), seq)],
        out_shape=[jax.ShapeDtypeStruct((PARTS, m, KV_DIM), BF16), jax.ShapeDtypeStruct((m, KV_DIM), F32),
                   jax.ShapeDtypeStruct((m, KV_DIM), F32), jax.ShapeDtypeStruct((PARTS, m, KV_DIM), F32),
                   jax.ShapeDtypeStruct((batch, WINDOW, KV_DIM), F32),
                   jax.ShapeDtypeStruct((batch, WINDOW, KV_DIM), F32),
                   jax.ShapeDtypeStruct((batch, hist, conv_ch), F32)],
        compiler_params=_params("arbitrary"),
        name="in_proj",
    )(x, g, w, b, cos, sup, sdn)


def _attn_scores(q, k_prev, k_own, v_prev, v_own, has_prev):
    blk = q.shape[0]
    kk = jnp.concatenate([k_prev, k_own], axis=0).astype(BF16)
    vv = jnp.concatenate([v_prev, v_own], axis=0).astype(BF16)
    lane_group = lax.broadcasted_iota(jnp.int32, kk.shape, 1) // HEAD_DIM
    zero = jnp.zeros_like(kk)
    kstack = jnp.concatenate([jnp.where(lane_group == g, kk, zero) for g in range(N_KV_HEADS)], axis=0)
    vstack = jnp.concatenate([jnp.where(lane_group == g, vv, zero) for g in range(N_KV_HEADS)], axis=0)
    qi = lax.broadcasted_iota(jnp.int32, (blk, 2 * blk), 0)
    kj = lax.broadcasted_iota(jnp.int32, (blk, 2 * blk), 1)
    rel = qi + blk - kj
    valid = (rel >= 0) & (rel < WINDOW) & (has_prev | (kj >= blk))
    return _dot_nt(q, kstack), valid, vstack


def _attn_probs(s_all, valid, sink_ref, part):
    width = valid.shape[1]
    probs = []
    for g in range(N_KV_HEADS):
        s = jnp.where(valid, s_all[:, g * width:(g + 1) * width], -jnp.inf)
        probs.append(_sink_softmax(s, sink_ref[g * GROUP + part]).astype(BF16))
    return jnp.concatenate(probs, axis=1)


def _conv_stage(u_own, halo, ue_ref, has_prev):
    ue_ref[0:CONV_HALO, :] = jnp.where(has_prev, halo, jnp.zeros_like(halo))
    ue_ref[CONV_HALO:, :] = u_own


def _conv_lanes(c, cw_ref, cb_ref, ue_ref, c_out):
    blk = ue_ref.shape[0] - CONV_HALO
    first = CONV_HALO - (CONV_W - 1)
    cs = slice(c * LANES, (c + 1) * LANES)
    out = cb_ref[:, cs]
    for r in range(SUBLANES):
        rows = blk + (SUBLANES if r else 0)
        part = None
        for a in range((first + CONV_W - 1 - r) // SUBLANES + 1):
            t = a * SUBLANES + r - first
            if 0 <= t < CONV_W:
                term = cw_ref[t:t + 1, cs] * ue_ref[a * SUBLANES:a * SUBLANES + rows, cs]
                part = term if part is None else part + term
        out = out + part[r:r + blk, :]
    c_out[:, cs] = out


def _mlp_mix_kernel(x_ref, q_ref, k_ref, kp_ref, v_ref, vp_ref, u_ref, uh_ref, sink_ref, cw_ref, cb_ref,
                    lg_ref, lb_ref, wo_ref, bo_ref, g2_ref, gf_ref, wu_ref, wd_ref,
                    o_ref, hn_ref, a_s, c_s, ue_ref, *, nb, nrow, final_norm):
    i, f = pl.program_id(0), pl.program_id(1)
    blk = WINDOW
    blocks_per_tile = x_ref.shape[0] // blk
    slot_mix, slot_mlp = i % 2, (i + 1) % 2

    @pl.when((i == 0) & (f == 0))
    def _():
        a_s[...] = jnp.zeros_like(a_s)
        c_s[...] = jnp.zeros_like(c_s)

    @pl.when(f == 0)
    def _():
        a = jnp.concatenate([a_s[slot_mlp, p] for p in range(PARTS)], axis=1)
        conv = jnp.concatenate([c_s[slot_mlp, p] for p in range(PARTS)], axis=1)
        c = _layer_norm_swish(conv, lg_ref[...], lb_ref[...]).astype(BF16)
        mix = _dot(a, wo_ref[0:Q_DIM, :]) + _dot(c, wo_ref[Q_DIM:, :])
        h = x_ref[...] + mix + bo_ref[...]
        o_ref[...] = h
        hn_ref[...] = _rms_norm_f32(h, g2_ref[...]).astype(hn_ref.dtype)

    sub, part = f // PARTS, f % PARTS
    gblk = jnp.minimum(i, nrow - 1) * blocks_per_tile + sub
    has_prev = gblk % nb != 0
    rows = pl.ds(pl.multiple_of(sub * blk, blk), blk)
    prev_rows = pl.ds(pl.multiple_of(jnp.maximum(sub - 1, 0) * blk, blk), blk)
    halo_rows = pl.ds(pl.multiple_of(jnp.maximum(sub * blk - CONV_HALO, 0), CONV_HALO), CONV_HALO)
    k_prev = jnp.where(sub == 0, kp_ref[...], k_ref[prev_rows, :])
    v_prev = jnp.where(sub == 0, vp_ref[...], v_ref[prev_rows, :])
    halo = jnp.where(sub == 0, uh_ref[part], u_ref[part, halo_rows, :])
    conv_out = c_s.at[slot_mix, part, rows, :]
    _conv_stage(u_ref[part, rows, :], halo, ue_ref, has_prev)
    s_all, valid, vstack = _attn_scores(q_ref[part, rows, :], k_prev, k_ref[rows, :], v_prev, v_ref[rows, :],
                                        has_prev)
    up = _dot(hn_ref[...], wu_ref[...])
    act = jnp.square(jnp.maximum(up, 0.0)).astype(BF16)
    _conv_lanes(0, cw_ref.at[part], cb_ref.at[part], ue_ref, conv_out)
    probs = _attn_probs(s_all, valid, sink_ref, part)
    _conv_lanes(1, cw_ref.at[part], cb_ref.at[part], ue_ref, conv_out)
    a_s[slot_mix, part, rows, :] = _dot(probs, vstack).astype(a_s.dtype)
    o_ref[...] += _dot(act, wd_ref[...])

    if final_norm:
        @pl.when(f == pl.num_programs(1) - 1)
        def _():
            o_ref[...] = _rms_norm_f32(o_ref[...], gf_ref[...])


def _mlp_mix(x, q, k, v, u, sinks, conv_w, conv_b, ln_g, ln_b, w_out, b_out, g2, gf, w_up, w_down,
             seq, tm, final_norm):
    m, d = x.shape
    conv_ch = u.shape[0] * u.shape[2]
    ff = w_up.shape[1]
    blk = WINDOW
    nb = seq // blk
    nrow = m // tm
    bpt = tm // blk
    nf = bpt * PARTS
    assert ff % nf == 0 and conv_ch == PARTS * KV_DIM and Q_DIM == PARTS * KV_DIM
    tf = ff // nf
    hpb = blk // CONV_HALO
    mix_row = lambda i: jnp.minimum(i, nrow - 1)
    mlp_tile = lambda i, f: (jnp.maximum(i - 1, 0), 0)
    mix_tile = lambda i, f: (mix_row(i), 0)
    mix_parts = lambda i, f: (0, mix_row(i), 0)
    prev = lambda i, f: (jnp.maximum(mix_row(i) * bpt - 1, 0), 0)
    halo_parts = lambda i, f: (0, jnp.maximum(mix_row(i) * bpt * hpb - 1, 0), 0)
    return pl.pallas_call(
        functools.partial(_mlp_mix_kernel, nb=nb, nrow=nrow, final_norm=final_norm),
        grid=(nrow + 1, nf),
        in_specs=[pl.BlockSpec((tm, d), mlp_tile),
                  pl.BlockSpec((PARTS, tm, KV_DIM), mix_parts),
                  pl.BlockSpec((tm, KV_DIM), mix_tile), pl.BlockSpec((blk, KV_DIM), prev),
                  pl.BlockSpec((tm, KV_DIM), mix_tile), pl.BlockSpec((blk, KV_DIM), prev),
                  pl.BlockSpec((PARTS, tm, KV_DIM), mix_parts),
                  pl.BlockSpec((PARTS, CONV_HALO, KV_DIM), halo_parts),
                  pl.BlockSpec(memory_space=pltpu.SMEM),
                  _resident((PARTS, CONV_W, KV_DIM)), _resident((PARTS, 1, KV_DIM)),
                  _resident((1, conv_ch)), _resident((1, conv_ch)),
                  _resident((Q_DIM + conv_ch, d)), _resident((1, d)), _resident((1, d)), _resident((1, d)),
                  pl.BlockSpec((d, tf), lambda i, f: (0, f)), pl.BlockSpec((tf, d), lambda i, f: (f, 0))],
        out_specs=pl.BlockSpec((tm, d), mlp_tile),
        out_shape=jax.ShapeDtypeStruct((m, d), F32),
        scratch_shapes=[pltpu.VMEM((tm, d), BF16),
                        pltpu.VMEM((2, PARTS, tm, KV_DIM), BF16),
                        pltpu.VMEM((2, PARTS, tm, KV_DIM), F32),
                        pltpu.VMEM((CONV_HALO + blk, KV_DIM), F32)],
        compiler_params=_params("arbitrary", "arbitrary"),
        name="mlp_mix",
    )(x, q, k, k, v, v, u, u, sinks, conv_w, conv_b, ln_g, ln_b, w_out, b_out, g2, gf, w_up, w_down)


def _in_proj_sample_kernel(x_ref, g_ref, wq_ref, bq_ref, wkv_ref, bkv_ref, wu_ref, bu_ref,
                           cos_ref, sup_ref, sdn_ref, cosc_ref, supc_ref, sdnc_ref,
                           q_ref, kt_ref, vt_ref, u_ref):
    conv_ch = u_ref.shape[1]
    hn = _rms_norm_f32(x_ref[...], g_ref[...]).astype(BF16)
    zq = _dot(hn, wq_ref[...]) + bq_ref[...]
    q_ref[...] = _rope_lanes(zq, cos_ref[...], sup_ref[...], sdn_ref[...]) * ATTN_SCALE
    zkv = _dot_nt(wkv_ref[...], hn) + bkv_ref[...]
    kt_ref[...] = _rope(zkv[0:KV_DIM, :], cosc_ref[...], supc_ref[...], sdnc_ref[...], 0)
    vt_ref[...] = zkv[KV_DIM:, :]
    zu = _dot(hn, wu_ref[:, 0:conv_ch]) + bu_ref[:, 0:conv_ch]
    zg = _dot(hn, wu_ref[:, conv_ch:]) + bu_ref[:, conv_ch:]
    u_ref[...] = zu * jax.nn.sigmoid(zg)


def _in_proj_sample(x, g, wq, bq, wkv_t, bkv_c, wu, bu, row_tabs, col_tabs):
    n, d = x.shape
    conv_ch = wu.shape[1] // 2
    full = lambda a: _resident(a.shape)
    args = (x, g, wq, bq, wkv_t, bkv_c, wu, bu, *row_tabs, *col_tabs)
    return pl.pallas_call(
        _in_proj_sample_kernel,
        grid=(1,),
        in_specs=[full(a) for a in args],
        out_specs=[_resident((n, Q_DIM)), _resident((KV_DIM, n)), _resident((KV_DIM, n)), _resident((n, conv_ch))],
        out_shape=[jax.ShapeDtypeStruct((n, Q_DIM), F32), jax.ShapeDtypeStruct((KV_DIM, n), F32),
                   jax.ShapeDtypeStruct((KV_DIM, n), F32), jax.ShapeDtypeStruct((n, conv_ch), F32)],
        compiler_params=_params("arbitrary"),
        name="in_proj_sample",
    )(*args)


def _attn_sample_kernel(q_ref, kn_ref, vn_ref, ck_ref, cv_ref, sink_ref, ao_ref, nk_ref, nv_ref):
    nseq, _, wb = ck_ref.shape
    rows = N_HEADS
    lane_group = lax.broadcasted_iota(jnp.int32, (rows, KV_DIM), 1) // HEAD_DIM
    row_group = lax.broadcasted_iota(jnp.int32, (rows, KV_DIM), 0) % N_KV_HEADS
    own_lanes = lane_group == row_group
    fold = (lax.broadcasted_iota(jnp.int32, (SUBLANES, rows), 1) // N_KV_HEADS
            == lax.broadcasted_iota(jnp.int32, (SUBLANES, rows), 0)).astype(F32)
    newest = lax.broadcasted_iota(jnp.int32, (KV_DIM, wb), 1) == wb - 1
    sink = sink_ref[...]
    kn, vn = kn_ref[0], vn_ref[0]

    for n in range(nseq):
        knew = jnp.where(newest, kn[:, n:n + 1], pltpu.roll(ck_ref[n], wb - 1, 1))
        vnew = jnp.where(newest, vn[:, n:n + 1], pltpu.roll(cv_ref[n], wb - 1, 1))
        nk_ref[n] = knew
        nv_ref[n] = vnew
        qrows = jnp.where(own_lanes, q_ref[n], jnp.zeros((rows, KV_DIM), F32))
        s = _dot(qrows.astype(BF16), knew.astype(BF16))
        o = _dot_nt(_sink_softmax(s, sink).astype(BF16), vnew.astype(BF16))
        o = jnp.where(own_lanes, o, jnp.zeros_like(o))
        ao_ref[n] = _dot(fold, o)


def _attn_sample(q_rep, kn, vn, cache_kt, cache_vt, sink_rows, nseq):
    n, _, wb = cache_kt.shape
    r3 = lambda s: (s, 0, 0)
    return pl.pallas_call(
        _attn_sample_kernel,
        grid=(n // nseq,),
        in_specs=[pl.BlockSpec((nseq, N_HEADS, KV_DIM), r3),
                  pl.BlockSpec((1, KV_DIM, nseq), r3), pl.BlockSpec((1, KV_DIM, nseq), r3),
                  pl.BlockSpec((nseq, KV_DIM, wb), r3), pl.BlockSpec((nseq, KV_DIM, wb), r3),
                  _resident((N_HEADS, 1))],
        out_specs=[pl.BlockSpec((nseq, SUBLANES, KV_DIM), r3),
                   pl.BlockSpec((nseq, KV_DIM, wb), r3), pl.BlockSpec((nseq, KV_DIM, wb), r3)],
        out_shape=[jax.ShapeDtypeStruct((n, SUBLANES, KV_DIM), F32),
                   jax.ShapeDtypeStruct((n, KV_DIM, wb), F32), jax.ShapeDtypeStruct((n, KV_DIM, wb), F32)],
        compiler_params=_params("arbitrary"),
        name="attn_sample",
    )(q_rep, kn, vn, cache_kt, cache_vt, sink_rows)


def _conv_sample_kernel(cc_ref, cn_ref, u_ref, cw_ref, cb_ref, lg_ref, lb_ref, co_ref, nc_ref, acc_ref, *, hist):
    j = pl.program_id(0)
    u = u_ref[...]

    @pl.when(j == 0)
    def _():
        acc_ref[...] = jnp.zeros_like(acc_ref)

    acc_ref[...] += cw_ref[j] * cc_ref[0]

    @pl.when(j < hist - 1)
    def _():
        nc_ref[0] = cn_ref[0]

    @pl.when(j == hist - 1)
    def _():
        nc_ref[0] = u
        y = acc_ref[...] + cw_ref[hist] * u + cb_ref[...]
        co_ref[...] = _layer_norm_swish(y, lg_ref[...], lb_ref[...]).astype(co_ref.dtype)


def _conv_sample(cache_t, u, conv_w, conv_b, ln_g, ln_b):
    hist, n, ch = cache_t.shape
    tap = lambda j: (j, 0, 0)
    return pl.pallas_call(
        functools.partial(_conv_sample_kernel, hist=hist),
        grid=(hist,),
        in_specs=[pl.BlockSpec((1, n, ch), tap),
                  pl.BlockSpec((1, n, ch), lambda j: (jnp.minimum(j + 1, hist - 1), 0, 0)),
                  _resident((n, ch)), _resident((hist + 1, 1, ch)),
                  _resident((1, ch)), _resident((1, ch)), _resident((1, ch))],
        out_specs=[_resident((n, ch)), pl.BlockSpec((1, n, ch), tap)],
        out_shape=[jax.ShapeDtypeStruct((n, ch), BF16), jax.ShapeDtypeStruct((hist, n, ch), F32)],
        scratch_shapes=[pltpu.VMEM((n, ch), F32)],
        compiler_params=_params("arbitrary"),
        name="conv_sample",
    )(cache_t, cache_t, u, conv_w.reshape(hist + 1, 1, ch), conv_b, ln_g, ln_b)


def _out_mlp_kernel(x_ref, a_ref, c_ref, wo_ref, bo_ref, g2_ref, wu_ref, wd_ref, gf_ref,
                    o_ref, hn_ref, *, final_norm):
    f = pl.program_id(1)

    @pl.when(f == 0)
    def _():
        mix = _dot(a_ref[...], wo_ref[0:Q_DIM, :]) + _dot(c_ref[...], wo_ref[Q_DIM:, :])
        h = x_ref[...] + mix + bo_ref[...]
        o_ref[...] = h
        hn_ref[...] = _rms_norm_f32(h, g2_ref[...]).astype(hn_ref.dtype)

    up = _dot(hn_ref[...], wu_ref[...])
    act = jnp.square(jnp.maximum(up, 0.0)).astype(BF16)
    o_ref[...] += _dot(act, wd_ref[...])

    if final_norm:
        @pl.when(f == pl.num_programs(1) - 1)
        def _():
            o_ref[...] = _rms_norm_f32(o_ref[...], gf_ref[...])


def _out_mlp(x, a, c, w_out, b_out, g2, w_up, w_down, gf, tm, tf, final_norm):
    m, d = x.shape
    mixw = w_out.shape[0]
    conv_ch = mixw - Q_DIM
    ff = w_up.shape[1]
    row = lambda i, f: (i, 0)
    return pl.pallas_call(
        functools.partial(_out_mlp_kernel, final_norm=final_norm),
        grid=(m // tm, ff // tf),
        in_specs=[pl.BlockSpec((tm, d), row), pl.BlockSpec((tm, Q_DIM), row),
                  pl.BlockSpec((tm, conv_ch), row),
                  _resident((mixw, d)), _resident((1, d)), _resident((1, d)),
                  pl.BlockSpec((d, tf), lambda i, f: (0, f)), pl.BlockSpec((tf, d), lambda i, f: (f, 0)),
                  _resident((1, d))],
        out_specs=pl.BlockSpec((tm, d), row),
        out_shape=jax.ShapeDtypeStruct((m, d), F32),
        scratch_shapes=[pltpu.VMEM((tm, d), BF16)],
        compiler_params=_params("arbitrary", "arbitrary"),
        name="out_mlp",
    )(x, a, c, w_out, b_out, g2, w_up, w_down, gf)


def _rope_tables(pos):
    half = ROT_DIM // 2
    inv_freq = jnp.power(jnp.float32(ROPE_THETA), -jnp.arange(half, dtype=jnp.float32) * 2.0 / ROT_DIM)
    ang = pos.astype(jnp.float32)[:, None] * inv_freq[None, :]
    cos, sin = jnp.cos(ang), jnp.sin(ang)
    t = pos.shape[0]
    pad = jnp.zeros((t, HEAD_DIM - ROT_DIM), F32)
    zeros = jnp.zeros((t, half), F32)
    per_head = lambda a, b, fill: jnp.concatenate([a, b, pad + fill], axis=1)
    reps = LANES // HEAD_DIM
    cos_t = jnp.tile(per_head(cos, cos, 1.0), (1, reps))
    sup_t = jnp.tile(per_head(zeros, sin, 0.0), (1, reps))
    sdn_t = jnp.tile(per_head(-sin, zeros, 0.0), (1, reps))
    return cos_t, sup_t, sdn_t


def _heads_to_igd(a, axis):
    shape = a.shape
    a = a.reshape(shape[:axis] + (N_KV_HEADS, GROUP, HEAD_DIM) + shape[axis + 1:])
    return jnp.swapaxes(a, axis, axis + 1).reshape(shape)


def _tile(m, target):
    t = min(m, target)
    assert m % t == 0, (m, t)
    return t


def kernel(x_prompt, x_sample, cache_k, cache_v, cache_conv, norm1_g, w_in, b_in, attn_sinks,
           conv_w, conv_b, conv_ln_g, conv_ln_b, w_out, b_out, norm2_g, w_up, w_down, final_norm_g):
    batch, seq, d_model = x_prompt.shape
    nsamp, t_s, _ = x_sample.shape
    depth = w_in.shape[0]
    conv_ch = conv_w.shape[2]
    wb = cache_k.shape[2]
    hist = CONV_W - 1
    assert t_s == 1 and seq % WINDOW == 0 and wb == WINDOW and cache_conv.shape[2] == hist
    assert depth >= 1

    tabs_p = _rope_tables(jnp.arange(seq, dtype=jnp.int32))
    tabs_s1 = _rope_tables(PAST_LEN + jnp.arange(t_s, dtype=jnp.int32))
    tabs_s = tuple(jnp.broadcast_to(t, (nsamp, LANES)) for t in tabs_s1)
    tabs_sc = tuple(jnp.tile(t[0], KV_DIM // LANES).reshape(KV_DIM, 1) for t in tabs_s1)
    row = lambda a: a.reshape(1, -1)

    hp = x_prompt.reshape(batch * seq, d_model)
    hs = x_sample.reshape(nsamp * t_s, d_model)
    tm_in = _tile(seq, 512)
    tm_mlp = _tile(batch * seq, 512)
    tf_s = _tile(w_up.shape[2], 1024)
    nseq = _tile(nsamp, 8)
    o1, o2 = Q_DIM, Q_DIM + 2 * KV_DIM
    pk, pv, pc, sk, sv, sc = [], [], [], [], [], []
    for l in range(depth):
        last = l == depth - 1
        wq = _heads_to_igd(w_in[l][:, :o1], 1).astype(BF16)
        bq = _heads_to_igd(b_in[l][:o1], 0)
        w_rest = w_in[l][:, o1:].astype(BF16)
        w_in_l = jnp.concatenate([wq, w_rest], axis=1)
        b_in_l = row(jnp.concatenate([bq, b_in[l][o1:]]))
        w_out_l = jnp.concatenate([_heads_to_igd(w_out[l][:Q_DIM], 0), w_out[l][Q_DIM:]], axis=0).astype(BF16)
        w_up_l, w_down_l = w_up[l].astype(BF16), w_down[l].astype(BF16)
        sinks = attn_sinks[l].astype(F32)
        sink_rows = sinks.reshape(N_KV_HEADS, GROUP).T.reshape(N_HEADS, 1)
        g1, g2, gf = row(norm1_g[l]), row(norm2_g[l]), row(final_norm_g)
        cb, lg, lb, bo = row(conv_b[l]), row(conv_ln_g[l]), row(conv_ln_b[l]), row(b_out[l])

        q, k, v, u, pk_l, pv_l, pc_l = _in_proj(hp, g1, w_in_l, b_in_l, *tabs_p, tm_in, batch)
        by_part = lambda a: a.reshape(a.shape[0], PARTS, KV_DIM).transpose(1, 0, 2)
        hp = _mlp_mix(hp, q, k, v, u, sinks, by_part(conv_w[l]), by_part(cb), lg, lb, w_out_l, bo, g2, gf,
                      w_up_l, w_down_l, seq, tm_mlp, last)
        pk.append(pk_l.reshape(batch, WINDOW, N_KV_HEADS, HEAD_DIM))
        pv.append(pv_l.reshape(batch, WINDOW, N_KV_HEADS, HEAD_DIM))
        pc.append(pc_l)

        cache_kt = jnp.transpose(cache_k[l], (0, 2, 3, 1)).reshape(nsamp, KV_DIM, wb)
        cache_vt = jnp.transpose(cache_v[l], (0, 2, 3, 1)).reshape(nsamp, KV_DIM, wb)
        cache_ct = jnp.transpose(cache_conv[l], (1, 0, 2))
        wkv_t = w_in[l][:, o1:o2].T.astype(BF16)
        bkv_c = b_in[l][o1:o2].reshape(2 * KV_DIM, 1)
        q, kt, vt, u = _in_proj_sample(hs, g1, wq, row(bq), wkv_t, bkv_c, w_rest[:, 2 * KV_DIM:],
                                       row(b_in[l][o2:]), tabs_s, tabs_sc)
        q_rep = jnp.repeat(q.reshape(nsamp, GROUP, KV_DIM), N_KV_HEADS, axis=1)
        by_block = lambda a: a.reshape(KV_DIM, nsamp // nseq, nseq).transpose(1, 0, 2)
        a8, nkt, nvt = _attn_sample(q_rep, by_block(kt), by_block(vt), cache_kt, cache_vt, sink_rows, nseq)
        c_o, nct = _conv_sample(cache_ct, u, conv_w[l], cb, lg, lb)
        a_o = a8[:, :GROUP, :].reshape(nsamp, Q_DIM).astype(BF16)
        hs = _out_mlp(hs, a_o, c_o, w_out_l, bo, g2, w_up_l, w_down_l, gf, nsamp, tf_s, last)
        to_cache = lambda a: jnp.transpose(a.reshape(nsamp, N_KV_HEADS, HEAD_DIM, wb), (0, 3, 1, 2))
        sk.append(to_cache(nkt))
        sv.append(to_cache(nvt))
        sc.append(jnp.transpose(nct, (1, 0, 2)))

    y_prompt = hp.reshape(batch, seq, d_model)
    y_sample = hs.reshape(nsamp, t_s, d_model)
    return (y_prompt, y_sample, jnp.stack(pk), jnp.stack(pv), jnp.stack(pc),
            jnp.stack(sk), jnp.stack(sv), jnp.stack(sc))
```

```python
import functools

import jax
import jax.numpy as jnp
import numpy as np
from jax import lax
from jax.experimental import pallas as pl
from jax.experimental.pallas import tpu as pltpu

HEAD_DIM = 64
N_HEADS = 16
N_KV_HEADS = 4
GROUP = N_HEADS // N_KV_HEADS
WINDOW = 128
ROT_DIM = HEAD_DIM // 4
ROPE_THETA = 500000.0
ATTN_SCALE = HEAD_DIM ** -0.5
Q_DIM = N_HEADS * HEAD_DIM
KV_DIM = N_KV_HEADS * HEAD_DIM
CONV_W = 31
PAST_LEN = 16384
EPS = 1e-5

LANES = 128
SUBLANES = 8
CONV_HALO = 32
CONV_ROWS = 32
VMEM_LIMIT_BYTES = 56 * 1024 * 1024
PARTS = GROUP
COL_CHUNK = 512

BF16 = jnp.bfloat16
F32 = jnp.float32


def _resident(shape):
    return pl.BlockSpec(shape, lambda *_: (0,) * len(shape), pipeline_mode=pl.Buffered(1))


def _params(*semantics):
    return pltpu.CompilerParams(dimension_semantics=semantics, vmem_limit_bytes=VMEM_LIMIT_BYTES)


def _rms_norm_f32(x, g):
    return x * lax.rsqrt(jnp.mean(x * x, axis=-1, keepdims=True) + EPS) * g


def _layer_norm_swish(y, g, b):
    mu = jnp.mean(y, axis=-1, keepdims=True)
    yc = y - mu
    var = jnp.mean(yc * yc, axis=-1, keepdims=True)
    yn = yc * lax.rsqrt(var + EPS) * g + b
    return yn * jax.nn.sigmoid(yn)


def _dot(a, b):
    return jnp.dot(a, b, preferred_element_type=F32)


def _dot_nt(a, b):
    return lax.dot_general(a, b, (((1,), (1,)), ((), ())), preferred_element_type=F32)


def _dot_chunks(a, w_ref, rows=slice(None)):
    return [_dot(a, w_ref[c, rows, :]) for c in range(w_ref.shape[0])]


def _sink_softmax(s, sink):
    m = jnp.maximum(jnp.max(s, axis=-1, keepdims=True), sink)
    p = jnp.exp(s - m)
    return p / (jnp.sum(p, axis=-1, keepdims=True) + jnp.exp(sink - m))


def _rope(z, cos, sin_up, sin_dn, axis):
    n = z.shape[axis]
    half = ROT_DIM // 2
    return z * cos + pltpu.roll(z, half, axis) * sin_up + pltpu.roll(z, n - half, axis) * sin_dn


def _rope_lanes(z, cos, sin_up, sin_dn):
    cols = [_rope(z[:, c * LANES:(c + 1) * LANES], cos, sin_up, sin_dn, 1) for c in range(z.shape[1] // LANES)]
    return jnp.concatenate(cols, axis=1)


def _in_proj_kernel(x_ref, g_ref, w_ref, b_ref, cos_ref, sup_ref, sdn_ref,
                    q_ref, k_ref, v_ref, u_ref, pk_ref, pv_ref, pc_ref):
    conv_ch = u_ref.shape[0] * u_ref.shape[2]
    tm = x_ref.shape[0]
    hist = pc_ref.shape[1]
    o1, o2, o3, o4 = Q_DIM, Q_DIM + KV_DIM, Q_DIM + 2 * KV_DIM, Q_DIM + 2 * KV_DIM + conv_ch
    hn = _rms_norm_f32(x_ref[...], g_ref[...]).astype(BF16)
    cos, sup, sdn = cos_ref[...], sup_ref[...], sdn_ref[...]

    zq = _dot(hn, w_ref[:, 0:o1]) + b_ref[:, 0:o1]
    q = (_rope_lanes(zq, cos, sup, sdn) * ATTN_SCALE).astype(q_ref.dtype)
    zk = _dot(hn, w_ref[:, o1:o2]) + b_ref[:, o1:o2]
    k_ref[...] = _rope_lanes(zk, cos, sup, sdn)
    v_ref[...] = _dot(hn, w_ref[:, o2:o3]) + b_ref[:, o2:o3]
    zu = _dot(hn, w_ref[:, o3:o4]) + b_ref[:, o3:o4]
    zg = _dot(hn, w_ref[:, o4:]) + b_ref[:, o4:]
    u = zu * jax.nn.sigmoid(zg)
    for p in range(PARTS):
        q_ref[p] = q[:, p * KV_DIM:(p + 1) * KV_DIM]
        u_ref[p] = u[:, p * KV_DIM:(p + 1) * KV_DIM]
    pk_ref[0] = k_ref[tm - WINDOW:, :]
    pv_ref[0] = v_ref[tm - WINDOW:, :]
    pc_ref[0] = u[tm - hist:, :]


def _in_proj(x, g, w, b, cos, sup, sdn, tm, batch):
    m, d = x.shape
    n = w.shape[1]
    conv_ch = (n - Q_DIM - 2 * KV_DIM) // 2
    nt = cos.shape[0] // tm
    hist = CONV_W - 1
    row = lambda i: (i, 0)
    parts = lambda i: (0, i, 0)
    tab = lambda i: (i % nt, 0)
    seq = lambda i: (i // nt, 0, 0)
    assert conv_ch == PARTS * KV_DIM and Q_DIM == PARTS * KV_DIM
    return pl.pallas_call(
        _in_proj_kernel,
        grid=(m // tm,),
        in_specs=[pl.BlockSpec((tm, d), row), _resident((1, d)), _resident((d, n)), _resident((1, n)),
                  pl.BlockSpec((tm, LANES), tab), pl.BlockSpec((tm, LANES), tab),
                  pl.BlockSpec((tm, LANES), tab)],
        out_specs=[pl.BlockSpec((PARTS, tm, KV_DIM), parts), pl.BlockSpec((tm, KV_DIM), row),
                   pl.BlockSpec((tm, KV_DIM), row), pl.BlockSpec((PARTS, tm, KV_DIM), parts),
                   pl.BlockSpec((1, WINDOW, KV_DIM), seq), pl.BlockSpec((1, WINDOW, KV_DIM), seq),
                   pl.BlockSpec((1, hist, conv_ch), seq)],
        out_shape=[jax.ShapeDtypeStruct((PARTS, m, KV_DIM), BF16), jax.ShapeDtypeStruct((m, KV_DIM), F32),
                   jax.ShapeDtypeStruct((m, KV_DIM), F32), jax.ShapeDtypeStruct((PARTS, m, KV_DIM), F32),
                   jax.ShapeDtypeStruct((batch, WINDOW, KV_DIM), F32),
                   jax.ShapeDtypeStruct((batch, WINDOW, KV_DIM), F32),
                   jax.ShapeDtypeStruct((batch, hist, conv_ch), F32)],
        compiler_params=_params("arbitrary"),
        name="in_proj",
    )(x, g, w, b, cos, sup, sdn)


def _attn_scores(q, k_prev, k_own, v_prev, v_own, has_prev):
    blk = q.shape[0]
    kk = jnp.concatenate([k_prev, k_own], axis=0).astype(BF16)
    vv = jnp.concatenate([v_prev, v_own], axis=0).astype(BF16)
    lane_group = lax.broadcasted_iota(jnp.int32, kk.shape, 1) // HEAD_DIM
    zero = jnp.zeros_like(kk)
    kstack = jnp.concatenate([jnp.where(lane_group == g, kk, zero) for g in range(N_KV_HEADS)], axis=0)
    vstack = jnp.concatenate([jnp.where(lane_group == g, vv, zero) for g in range(N_KV_HEADS)], axis=0)
    qi = lax.broadcasted_iota(jnp.int32, (blk, 2 * blk), 0)
    kj = lax.broadcasted_iota(jnp.int32, (blk, 2 * blk), 1)
    rel = qi + blk - kj
    valid = (rel >= 0) & (rel < WINDOW) & (has_prev | (kj >= blk))
    return _dot_nt(q, kstack), valid, vstack


def _attn_probs(s_all, valid, sink_ref, part, zero_tile):
    width = valid.shape[1]
    probs = []
    for g in range(N_KV_HEADS):
        s = jnp.where(valid, s_all[:, g * width:(g + 1) * width], -jnp.inf)
        p = _sink_softmax(s, sink_ref[g * GROUP + part])
        if g == 0:
            top = jnp.concatenate([p[0:SUBLANES, 0:LANES] + zero_tile, p[0:SUBLANES, LANES:]], axis=1)
            p = jnp.concatenate([top, p[SUBLANES:, :]], axis=0)
        probs.append(p.astype(BF16))
    return jnp.concatenate(probs, axis=1)


def _conv_stage(u_own, halo, ue_ref, has_prev):
    ue_ref[0:CONV_HALO, :] = jnp.where(has_prev, halo, jnp.zeros_like(halo))
    ue_ref[CONV_HALO:, :] = u_own


def _conv_lanes(c, cw_ref, cb_ref, ue_ref, c_out):
    blk = ue_ref.shape[0] - CONV_HALO
    first = CONV_HALO - (CONV_W - 1)
    cs = slice(c * LANES, (c + 1) * LANES)
    seen = None
    for s0 in range(0, blk, CONV_ROWS):
        out = cb_ref[:, cs]
        for r in range(SUBLANES):
            rows = CONV_ROWS + (SUBLANES if r else 0)
            part = None
            for a in range((first + CONV_W - 1 - r) // SUBLANES + 1):
                t = a * SUBLANES + r - first
                if 0 <= t < CONV_W:
                    lo = s0 + a * SUBLANES
                    term = cw_ref[t:t + 1, cs] * ue_ref[lo:lo + rows, cs]
                    part = term if part is None else part + term
            out = out + part[r:r + CONV_ROWS, :]
        c_out[s0:s0 + CONV_ROWS, cs] = out
        for r0 in range(0, CONV_ROWS, SUBLANES):
            tile = out[r0:r0 + SUBLANES, :]
            seen = tile if seen is None else jnp.maximum(seen, tile)
    return seen


def _mlp_mix_kernel(x_ref, q_ref, k_ref, kp_ref, v_ref, vp_ref, u_ref, uh_ref, sink_ref, cw_ref, cb_ref,
                    lg_ref, lb_ref, wo_ref, bo_ref, g2_ref, gf_ref, wu_ref, wd_ref,
                    o_ref, hn_ref, a_s, c_s, ue_ref, *, nb, nrow, final_norm):
    i, f = pl.program_id(0), pl.program_id(1)
    blk = WINDOW
    blocks_per_tile = x_ref.shape[0] // blk
    slot_mix, slot_mlp = i % 2, (i + 1) % 2

    @pl.when((i == 0) & (f == 0))
    def _():
        a_s[...] = jnp.zeros_like(a_s)
        c_s[...] = jnp.zeros_like(c_s)

    @pl.when(f == 0)
    def _():
        a = jnp.concatenate([a_s[slot_mlp, p] for p in range(PARTS)], axis=1)
        conv = jnp.concatenate([c_s[slot_mlp, p] for p in range(PARTS)], axis=1)
        c = _layer_norm_swish(conv, lg_ref[...], lb_ref[...]).astype(BF16)
        mix = [ma + mc for ma, mc in zip(_dot_chunks(a, wo_ref, slice(0, Q_DIM)),
                                         _dot_chunks(c, wo_ref, slice(Q_DIM, None)))]
        h = x_ref[...] + jnp.concatenate(mix, axis=1) + bo_ref[...]
        o_ref[...] = h
        hn_ref[...] = _rms_norm_f32(h, g2_ref[...]).astype(hn_ref.dtype)

    sub, part = f // PARTS, f % PARTS
    gblk = jnp.minimum(i, nrow - 1) * blocks_per_tile + sub
    has_prev = gblk % nb != 0
    rows = pl.ds(pl.multiple_of(sub * blk, blk), blk)
    prev_rows = pl.ds(pl.multiple_of(jnp.maximum(sub - 1, 0) * blk, blk), blk)
    halo_rows = pl.ds(pl.multiple_of(jnp.maximum(sub * blk - CONV_HALO, 0), CONV_HALO), CONV_HALO)
    k_prev = jnp.where(sub == 0, kp_ref[...], k_ref[prev_rows, :])
    v_prev = jnp.where(sub == 0, vp_ref[...], v_ref[prev_rows, :])
    halo = jnp.where(sub == 0, uh_ref[part], u_ref[part, halo_rows, :])
    conv_out = c_s.at[slot_mix, part, rows, :]
    _conv_stage(u_ref[part, rows, :], halo, ue_ref, has_prev)
    seen = jnp.maximum(_conv_lanes(0, cw_ref.at[part], cb_ref.at[part], ue_ref, conv_out),
                       _conv_lanes(1, cw_ref.at[part], cb_ref.at[part], ue_ref, conv_out))
    conv_done = jnp.minimum(jnp.abs(seen), 0.0)
    s_all, valid, vstack = _attn_scores(q_ref[part, rows, :], k_prev, k_ref[rows, :], v_prev, v_ref[rows, :],
                                        has_prev)
    up = _dot(hn_ref[...], wu_ref[...])
    act = jnp.square(jnp.maximum(up, 0.0)).astype(BF16)
    probs = _attn_probs(s_all, valid, sink_ref, part, conv_done)
    a_s[slot_mix, part, rows, :] = _dot(probs, vstack).astype(a_s.dtype)
    for c, down in enumerate(_dot_chunks(act, wd_ref)):
        o_ref[:, c * COL_CHUNK:(c + 1) * COL_CHUNK] += down

    if final_norm:
        @pl.when(f == pl.num_programs(1) - 1)
        def _():
            o_ref[...] = _rms_norm_f32(o_ref[...], gf_ref[...])


def _mlp_mix(x, q, k, v, u, sinks, conv_w, conv_b, ln_g, ln_b, w_out, b_out, g2, gf, w_up, w_down,
             seq, tm, final_norm):
    m, d = x.shape
    conv_ch = u.shape[0] * u.shape[2]
    ff = w_up.shape[1]
    blk = WINDOW
    nb = seq // blk
    nrow = m // tm
    bpt = tm // blk
    nf = bpt * PARTS
    assert ff % nf == 0 and conv_ch == PARTS * KV_DIM and Q_DIM == PARTS * KV_DIM
    tf = ff // nf
    hpb = blk // CONV_HALO
    mix_row = lambda i: jnp.minimum(i, nrow - 1)
    mlp_tile = lambda i, f: (jnp.maximum(i - 1, 0), 0)
    mix_tile = lambda i, f: (mix_row(i), 0)
    mix_parts = lambda i, f: (0, mix_row(i), 0)
    prev = lambda i, f: (jnp.maximum(mix_row(i) * bpt - 1, 0), 0)
    halo_parts = lambda i, f: (0, jnp.maximum(mix_row(i) * bpt * hpb - 1, 0), 0)
    return pl.pallas_call(
        functools.partial(_mlp_mix_kernel, nb=nb, nrow=nrow, final_norm=final_norm),
        grid=(nrow + 1, nf),
        in_specs=[pl.BlockSpec((tm, d), mlp_tile),
                  pl.BlockSpec((PARTS, tm, KV_DIM), mix_parts),
                  pl.BlockSpec((tm, KV_DIM), mix_tile), pl.BlockSpec((blk, KV_DIM), prev),
                  pl.BlockSpec((tm, KV_DIM), mix_tile), pl.BlockSpec((blk, KV_DIM), prev),
                  pl.BlockSpec((PARTS, tm, KV_DIM), mix_parts),
                  pl.BlockSpec((PARTS, CONV_HALO, KV_DIM), halo_parts),
                  pl.BlockSpec(memory_space=pltpu.SMEM),
                  _resident((PARTS, CONV_W, KV_DIM)), _resident((PARTS, 1, KV_DIM)),
                  _resident((1, conv_ch)), _resident((1, conv_ch)),
                  _resident(w_out.shape), _resident((1, d)), _resident((1, d)), _resident((1, d)),
                  pl.BlockSpec((d, tf), lambda i, f: (0, f)),
                  pl.BlockSpec((w_down.shape[0], tf, COL_CHUNK), lambda i, f: (0, f, 0))],
        out_specs=pl.BlockSpec((tm, d), mlp_tile),
        out_shape=jax.ShapeDtypeStruct((m, d), F32),
        scratch_shapes=[pltpu.VMEM((tm, d), BF16),
                        pltpu.VMEM((2, PARTS, tm, KV_DIM), BF16),
                        pltpu.VMEM((2, PARTS, tm, KV_DIM), F32),
                        pltpu.VMEM((CONV_HALO + blk, KV_DIM), F32)],
        compiler_params=_params("arbitrary", "arbitrary"),
        name="mlp_mix",
    )(x, q, k, k, v, v, u, u, sinks, conv_w, conv_b, ln_g, ln_b, w_out, b_out, g2, gf, w_up, w_down)


def _in_proj_sample_kernel(x_ref, g_ref, wq_ref, bq_ref, wkv_ref, bkv_ref, wu_ref, bu_ref,
                           cos_ref, sup_ref, sdn_ref, cosc_ref, supc_ref, sdnc_ref,
                           q_ref, kt_ref, vt_ref, u_ref):
    conv_ch = u_ref.shape[1]
    hn = _rms_norm_f32(x_ref[...], g_ref[...]).astype(BF16)
    zq = _dot(hn, wq_ref[...]) + bq_ref[...]
    q_ref[...] = _rope_lanes(zq, cos_ref[...], sup_ref[...], sdn_ref[...]) * ATTN_SCALE
    zkv = _dot_nt(wkv_ref[...], hn) + bkv_ref[...]
    kt_ref[...] = _rope(zkv[0:KV_DIM, :], cosc_ref[...], supc_ref[...], sdnc_ref[...], 0)
    vt_ref[...] = zkv[KV_DIM:, :]
    zu = _dot(hn, wu_ref[:, 0:conv_ch]) + bu_ref[:, 0:conv_ch]
    zg = _dot(hn, wu_ref[:, conv_ch:]) + bu_ref[:, conv_ch:]
    u_ref[...] = zu * jax.nn.sigmoid(zg)


def _in_proj_sample(x, g, wq, bq, wkv_t, bkv_c, wu, bu, row_tabs, col_tabs):
    n, d = x.shape
    conv_ch = wu.shape[1] // 2
    full = lambda a: _resident(a.shape)
    args = (x, g, wq, bq, wkv_t, bkv_c, wu, bu, *row_tabs, *col_tabs)
    return pl.pallas_call(
        _in_proj_sample_kernel,
        grid=(1,),
        in_specs=[full(a) for a in args],
        out_specs=[_resident((n, Q_DIM)), _resident((KV_DIM, n)), _resident((KV_DIM, n)), _resident((n, conv_ch))],
        out_shape=[jax.ShapeDtypeStruct((n, Q_DIM), F32), jax.ShapeDtypeStruct((KV_DIM, n), F32),
                   jax.ShapeDtypeStruct((KV_DIM, n), F32), jax.ShapeDtypeStruct((n, conv_ch), F32)],
        compiler_params=_params("arbitrary"),
        name="in_proj_sample",
    )(*args)


def _attn_sample_kernel(q_ref, kn_ref, vn_ref, ck_ref, cv_ref, sink_ref, ao_ref, nk_ref, nv_ref):
    nseq, _, wb = ck_ref.shape
    rows = N_HEADS
    lane_group = lax.broadcasted_iota(jnp.int32, (rows, KV_DIM), 1) // HEAD_DIM
    row_group = lax.broadcasted_iota(jnp.int32, (rows, KV_DIM), 0) % N_KV_HEADS
    own_lanes = lane_group == row_group
    fold = (lax.broadcasted_iota(jnp.int32, (SUBLANES, rows), 1) // N_KV_HEADS
            == lax.broadcasted_iota(jnp.int32, (SUBLANES, rows), 0)).astype(F32)
    newest = lax.broadcasted_iota(jnp.int32, (KV_DIM, wb), 1) == wb - 1
    sink = sink_ref[...]
    kn, vn = kn_ref[0], vn_ref[0]

    for n in range(nseq):
        knew = jnp.where(newest, kn[:, n:n + 1], pltpu.roll(ck_ref[n], wb - 1, 1))
        vnew = jnp.where(newest, vn[:, n:n + 1], pltpu.roll(cv_ref[n], wb - 1, 1))
        nk_ref[n] = knew
        nv_ref[n] = vnew
        qrows = jnp.where(own_lanes, q_ref[n], jnp.zeros((rows, KV_DIM), F32))
        s = _dot(qrows.astype(BF16), knew.astype(BF16))
        o = _dot_nt(_sink_softmax(s, sink).astype(BF16), vnew.astype(BF16))
        o = jnp.where(own_lanes, o, jnp.zeros_like(o))
        ao_ref[n] = _dot(fold, o)


def _attn_sample(q_rep, kn, vn, cache_kt, cache_vt, sink_rows, nseq):
    n, _, wb = cache_kt.shape
    r3 = lambda s: (s, 0, 0)
    return pl.pallas_call(
        _attn_sample_kernel,
        grid=(n // nseq,),
        in_specs=[pl.BlockSpec((nseq, N_HEADS, KV_DIM), r3),
                  pl.BlockSpec((1, KV_DIM, nseq), r3), pl.BlockSpec((1, KV_DIM, nseq), r3),
                  pl.BlockSpec((nseq, KV_DIM, wb), r3), pl.BlockSpec((nseq, KV_DIM, wb), r3),
                  _resident((N_HEADS, 1))],
        out_specs=[pl.BlockSpec((nseq, SUBLANES, KV_DIM), r3),
                   pl.BlockSpec((nseq, KV_DIM, wb), r3), pl.BlockSpec((nseq, KV_DIM, wb), r3)],
        out_shape=[jax.ShapeDtypeStruct((n, SUBLANES, KV_DIM), F32),
                   jax.ShapeDtypeStruct((n, KV_DIM, wb), F32), jax.ShapeDtypeStruct((n, KV_DIM, wb), F32)],
        compiler_params=_params("arbitrary"),
        name="attn_sample",
    )(q_rep, kn, vn, cache_kt, cache_vt, sink_rows)


def _conv_sample_kernel(cc_ref, cn_ref, u_ref, cw_ref, cb_ref, lg_ref, lb_ref, co_ref, nc_ref, acc_ref, *, hist):
    j = pl.program_id(0)
    u = u_ref[...]

    @pl.when(j == 0)
    def _():
        acc_ref[...] = jnp.zeros_like(acc_ref)

    acc_ref[...] += cw_ref[j] * cc_ref[0]

    @pl.when(j < hist - 1)
    def _():
        nc_ref[0] = cn_ref[0]

    @pl.when(j == hist - 1)
    def _():
        nc_ref[0] = u
        y = acc_ref[...] + cw_ref[hist] * u + cb_ref[...]
        co_ref[...] = _layer_norm_swish(y, lg_ref[...], lb_ref[...]).astype(co_ref.dtype)


def _conv_sample(cache_t, u, conv_w, conv_b, ln_g, ln_b):
    hist, n, ch = cache_t.shape
    tap = lambda j: (j, 0, 0)
    return pl.pallas_call(
        functools.partial(_conv_sample_kernel, hist=hist),
        grid=(hist,),
        in_specs=[pl.BlockSpec((1, n, ch), tap),
                  pl.BlockSpec((1, n, ch), lambda j: (jnp.minimum(j + 1, hist - 1), 0, 0)),
                  _resident((n, ch)), _resident((hist + 1, 1, ch)),
                  _resident((1, ch)), _resident((1, ch)), _resident((1, ch))],
        out_specs=[_resident((n, ch)), pl.BlockSpec((1, n, ch), tap)],
        out_shape=[jax.ShapeDtypeStruct((n, ch), BF16), jax.ShapeDtypeStruct((hist, n, ch), F32)],
        scratch_shapes=[pltpu.VMEM((n, ch), F32)],
        compiler_params=_params("arbitrary"),
        name="conv_sample",
    )(cache_t, cache_t, u, conv_w.reshape(hist + 1, 1, ch), conv_b, ln_g, ln_b)


def _out_mlp_kernel(x_ref, a_ref, c_ref, wo_ref, bo_ref, g2_ref, wu_ref, wd_ref, gf_ref,
                    o_ref, hn_ref, *, final_norm):
    f = pl.program_id(1)

    @pl.when(f == 0)
    def _():
        mix = [ma + mc for ma, mc in zip(_dot_chunks(a_ref[...], wo_ref, slice(0, Q_DIM)),
                                         _dot_chunks(c_ref[...], wo_ref, slice(Q_DIM, None)))]
        h = x_ref[...] + jnp.concatenate(mix, axis=1) + bo_ref[...]
        o_ref[...] = h
        hn_ref[...] = _rms_norm_f32(h, g2_ref[...]).astype(hn_ref.dtype)

    up = _dot(hn_ref[...], wu_ref[...])
    act = jnp.square(jnp.maximum(up, 0.0)).astype(BF16)
    for c, down in enumerate(_dot_chunks(act, wd_ref)):
        o_ref[:, c * COL_CHUNK:(c + 1) * COL_CHUNK] += down

    if final_norm:
        @pl.when(f == pl.num_programs(1) - 1)
        def _():
            o_ref[...] = _rms_norm_f32(o_ref[...], gf_ref[...])


def _out_mlp(x, a, c, w_out, b_out, g2, w_up, w_down, gf, tm, tf, final_norm):
    m, d = x.shape
    conv_ch = w_out.shape[1] - Q_DIM
    ff = w_up.shape[1]
    row = lambda i, f: (i, 0)
    return pl.pallas_call(
        functools.partial(_out_mlp_kernel, final_norm=final_norm),
        grid=(m // tm, ff // tf),
        in_specs=[pl.BlockSpec((tm, d), row), pl.BlockSpec((tm, Q_DIM), row),
                  pl.BlockSpec((tm, conv_ch), row),
                  _resident(w_out.shape), _resident((1, d)), _resident((1, d)),
                  pl.BlockSpec((d, tf), lambda i, f: (0, f)),
                  pl.BlockSpec((w_down.shape[0], tf, COL_CHUNK), lambda i, f: (0, f, 0)),
                  _resident((1, d))],
        out_specs=pl.BlockSpec((tm, d), row),
        out_shape=jax.ShapeDtypeStruct((m, d), F32),
        scratch_shapes=[pltpu.VMEM((tm, d), BF16)],
        compiler_params=_params("arbitrary", "arbitrary"),
        name="out_mlp",
    )(x, a, c, w_out, b_out, g2, w_up, w_down, gf)


def _rope_tables(pos):
    half = ROT_DIM // 2
    inv_freq = jnp.power(jnp.float32(ROPE_THETA), -jnp.arange(half, dtype=jnp.float32) * 2.0 / ROT_DIM)
    ang = pos.astype(jnp.float32)[:, None] * inv_freq[None, :]
    cos, sin = jnp.cos(ang), jnp.sin(ang)
    t = pos.shape[0]
    pad = jnp.zeros((t, HEAD_DIM - ROT_DIM), F32)
    zeros = jnp.zeros((t, half), F32)
    per_head = lambda a, b, fill: jnp.concatenate([a, b, pad + fill], axis=1)
    reps = LANES // HEAD_DIM
    cos_t = jnp.tile(per_head(cos, cos, 1.0), (1, reps))
    sup_t = jnp.tile(per_head(zeros, sin, 0.0), (1, reps))
    sdn_t = jnp.tile(per_head(-sin, zeros, 0.0), (1, reps))
    return cos_t, sup_t, sdn_t


def _heads_to_igd(a, axis):
    shape = a.shape
    a = a.reshape(shape[:axis] + (N_KV_HEADS, GROUP, HEAD_DIM) + shape[axis + 1:])
    return jnp.swapaxes(a, axis, axis + 1).reshape(shape)


def _col_chunks(w):
    k, n = w.shape
    return w.reshape(k, n // COL_CHUNK, COL_CHUNK).transpose(1, 0, 2)


def _tile(m, target):
    t = min(m, target)
    assert m % t == 0, (m, t)
    return t


def kernel(x_prompt, x_sample, cache_k, cache_v, cache_conv, norm1_g, w_in, b_in, attn_sinks,
           conv_w, conv_b, conv_ln_g, conv_ln_b, w_out, b_out, norm2_g, w_up, w_down, final_norm_g):
    batch, seq, d_model = x_prompt.shape
    nsamp, t_s, _ = x_sample.shape
    depth = w_in.shape[0]
    conv_ch = conv_w.shape[2]
    wb = cache_k.shape[2]
    hist = CONV_W - 1
    assert t_s == 1 and seq % WINDOW == 0 and wb == WINDOW and cache_conv.shape[2] == hist
    assert depth >= 1

    tabs_p = _rope_tables(jnp.arange(seq, dtype=jnp.int32))
    tabs_s1 = _rope_tables(PAST_LEN + jnp.arange(t_s, dtype=jnp.int32))
    tabs_s = tuple(jnp.broadcast_to(t, (nsamp, LANES)) for t in tabs_s1)
    tabs_sc = tuple(jnp.tile(t[0], KV_DIM // LANES).reshape(KV_DIM, 1) for t in tabs_s1)
    row = lambda a: a.reshape(1, -1)

    hp = x_prompt.reshape(batch * seq, d_model)
    hs = x_sample.reshape(nsamp * t_s, d_model)
    tm_in = _tile(seq, 512)
    tm_mlp = _tile(batch * seq, 512)
    tf_s = _tile(w_up.shape[2], 1024)
    nseq = _tile(nsamp, 8)
    o1, o2 = Q_DIM, Q_DIM + 2 * KV_DIM
    pk, pv, pc, sk, sv, sc = [], [], [], [], [], []
    for l in range(depth):
        last = l == depth - 1
        wq = _heads_to_igd(w_in[l][:, :o1], 1).astype(BF16)
        bq = _heads_to_igd(b_in[l][:o1], 0)
        w_rest = w_in[l][:, o1:].astype(BF16)
        w_in_l = jnp.concatenate([wq, w_rest], axis=1)
        b_in_l = row(jnp.concatenate([bq, b_in[l][o1:]]))
        w_out_l = _col_chunks(
            jnp.concatenate([_heads_to_igd(w_out[l][:Q_DIM], 0), w_out[l][Q_DIM:]], axis=0).astype(BF16))
        w_up_l, w_down_l = w_up[l].astype(BF16), _col_chunks(w_down[l].astype(BF16))
        sinks = attn_sinks[l].astype(F32)
        sink_rows = sinks.reshape(N_KV_HEADS, GROUP).T.reshape(N_HEADS, 1)
        g1, g2, gf = row(norm1_g[l]), row(norm2_g[l]), row(final_norm_g)
        cb, lg, lb, bo = row(conv_b[l]), row(conv_ln_g[l]), row(conv_ln_b[l]), row(b_out[l])

        q, k, v, u, pk_l, pv_l, pc_l = _in_proj(hp, g1, w_in_l, b_in_l, *tabs_p, tm_in, batch)
        by_part = lambda a: a.reshape(a.shape[0], PARTS, KV_DIM).transpose(1, 0, 2)
        hp = _mlp_mix(hp, q, k, v, u, sinks, by_part(conv_w[l]), by_part(cb), lg, lb, w_out_l, bo, g2, gf,
                      w_up_l, w_down_l, seq, tm_mlp, last)
        pk.append(pk_l.reshape(batch, WINDOW, N_KV_HEADS, HEAD_DIM))
        pv.append(pv_l.reshape(batch, WINDOW, N_KV_HEADS, HEAD_DIM))
        pc.append(pc_l)

        cache_kt = jnp.transpose(cache_k[l], (0, 2, 3, 1)).reshape(nsamp, KV_DIM, wb)
        cache_vt = jnp.transpose(cache_v[l], (0, 2, 3, 1)).reshape(nsamp, KV_DIM, wb)
        cache_ct = jnp.transpose(cache_conv[l], (1, 0, 2))
        wkv_t = w_in[l][:, o1:o2].T.astype(BF16)
        bkv_c = b_in[l][o1:o2].reshape(2 * KV_DIM, 1)
        q, kt, vt, u = _in_proj_sample(hs, g1, wq, row(bq), wkv_t, bkv_c, w_rest[:, 2 * KV_DIM:],
                                       row(b_in[l][o2:]), tabs_s, tabs_sc)
        q_rep = jnp.repeat(q.reshape(nsamp, GROUP, KV_DIM), N_KV_HEADS, axis=1)
        by_block = lambda a: a.reshape(KV_DIM, nsamp // nseq, nseq).transpose(1, 0, 2)
        a8, nkt, nvt = _attn_sample(q_rep, by_block(kt), by_block(vt), cache_kt, cache_vt, sink_rows, nseq)
        c_o, nct = _conv_sample(cache_ct, u, conv_w[l], cb, lg, lb)
        a_o = a8[:, :GROUP, :].reshape(nsamp, Q_DIM).astype(BF16)
        hs = _out_mlp(hs, a_o, c_o, w_out_l, bo, g2, w_up_l, w_down_l, gf, nsamp, tf_s, last)
        to_cache = lambda a: jnp.transpose(a.reshape(nsamp, N_KV_HEADS, HEAD_DIM, wb), (0, 3, 1, 2))
        sk.append(to_cache(nkt))
        sv.append(to_cache(nvt))
        sc.append(jnp.transpose(nct, (1, 0, 2)))

    y_prompt = hp.reshape(batch, seq, d_model)
    y_sample = hs.reshape(nsamp, t_s, d_model)
    return (y_prompt, y_sample, jnp.stack(pk), jnp.stack(pv), jnp.stack(pc),
            jnp.stack(sk), jnp.stack(sv), jnp.stack(sc))
```

```python
import functools

import jax
import jax.numpy as jnp
import numpy as np
from jax import lax
from jax.experimental import pallas as pl
from jax.experimental.pallas import tpu as pltpu

HEAD_DIM = 64
N_HEADS = 16
N_KV_HEADS = 4
GROUP = N_HEADS // N_KV_HEADS
WINDOW = 128
ROT_DIM = HEAD_DIM // 4
ROPE_THETA = 500000.0
ATTN_SCALE = HEAD_DIM ** -0.5
Q_DIM = N_HEADS * HEAD_DIM
KV_DIM = N_KV_HEADS * HEAD_DIM
CONV_W = 31
PAST_LEN = 16384
EPS = 1e-5

LANES = 128
SUBLANES = 8
CONV_HALO = 32
CONV_ROWS = 32
VMEM_LIMIT_BYTES = 60 * 1024 * 1024
PARTS = GROUP
COL_CHUNK = 512

BF16 = jnp.bfloat16
F32 = jnp.float32


def _resident(shape):
    return pl.BlockSpec(shape, lambda *_: (0,) * len(shape), pipeline_mode=pl.Buffered(1))


def _params(*semantics):
    return pltpu.CompilerParams(dimension_semantics=semantics, vmem_limit_bytes=VMEM_LIMIT_BYTES)


def _rms_norm_f32(x, g):
    return x * lax.rsqrt(jnp.mean(x * x, axis=-1, keepdims=True) + EPS) * g


def _layer_norm_swish(y, g, b):
    mu = jnp.mean(y, axis=-1, keepdims=True)
    yc = y - mu
    var = jnp.mean(yc * yc, axis=-1, keepdims=True)
    yn = yc * lax.rsqrt(var + EPS) * g + b
    return yn * jax.nn.sigmoid(yn)


def _dot(a, b):
    return jnp.dot(a, b, preferred_element_type=F32)


def _dot_nt(a, b):
    return lax.dot_general(a, b, (((1,), (1,)), ((), ())), preferred_element_type=F32)


def _dot_chunks(a, w_ref, rows=slice(None)):
    return [_dot(a, w_ref[c, rows, :]) for c in range(w_ref.shape[0])]


def _sink_softmax(s, sink):
    m = jnp.maximum(jnp.max(s, axis=-1, keepdims=True), sink)
    p = jnp.exp(s - m)
    return p / (jnp.sum(p, axis=-1, keepdims=True) + jnp.exp(sink - m))


def _rope(z, cos, sin_up, sin_dn, axis):
    n = z.shape[axis]
    half = ROT_DIM // 2
    return z * cos + pltpu.roll(z, half, axis) * sin_up + pltpu.roll(z, n - half, axis) * sin_dn


def _rope_lanes(z, cos, sin_up, sin_dn):
    cols = [_rope(z[:, c * LANES:(c + 1) * LANES], cos, sin_up, sin_dn, 1) for c in range(z.shape[1] // LANES)]
    return jnp.concatenate(cols, axis=1)


def _in_proj_kernel(x_ref, g_ref, w_ref, b_ref, cos_ref, sup_ref, sdn_ref,
                    q_ref, k_ref, v_ref, u_ref, pk_ref, pv_ref, pc_ref):
    conv_ch = u_ref.shape[0] * u_ref.shape[2]
    tm = x_ref.shape[0]
    hist = pc_ref.shape[1]
    o1, o2, o3, o4 = Q_DIM, Q_DIM + KV_DIM, Q_DIM + 2 * KV_DIM, Q_DIM + 2 * KV_DIM + conv_ch
    hn = _rms_norm_f32(x_ref[...], g_ref[...]).astype(BF16)
    cos, sup, sdn = cos_ref[...], sup_ref[...], sdn_ref[...]

    zq = _dot(hn, w_ref[:, 0:o1]) + b_ref[:, 0:o1]
    q = (_rope_lanes(zq, cos, sup, sdn) * ATTN_SCALE).astype(q_ref.dtype)
    zk = _dot(hn, w_ref[:, o1:o2]) + b_ref[:, o1:o2]
    k_ref[...] = _rope_lanes(zk, cos, sup, sdn)
    v_ref[...] = _dot(hn, w_ref[:, o2:o3]) + b_ref[:, o2:o3]
    zu = _dot(hn, w_ref[:, o3:o4]) + b_ref[:, o3:o4]
    zg = _dot(hn, w_ref[:, o4:]) + b_ref[:, o4:]
    u = zu * jax.nn.sigmoid(zg)
    for p in range(PARTS):
        q_ref[p] = q[:, p * KV_DIM:(p + 1) * KV_DIM]
        u_ref[p] = u[:, p * KV_DIM:(p + 1) * KV_DIM]
    pk_ref[0] = k_ref[tm - WINDOW:, :]
    pv_ref[0] = v_ref[tm - WINDOW:, :]
    pc_ref[0] = u[tm - hist:, :]


def _in_proj(x, g, w, b, cos, sup, sdn, tm, batch):
    m, d = x.shape
    n = w.shape[1]
    conv_ch = (n - Q_DIM - 2 * KV_DIM) // 2
    nt = cos.shape[0] // tm
    hist = CONV_W - 1
    row = lambda i: (i, 0)
    parts = lambda i: (0, i, 0)
    tab = lambda i: (i % nt, 0)
    seq = lambda i: (i // nt, 0, 0)
    assert conv_ch == PARTS * KV_DIM and Q_DIM == PARTS * KV_DIM
    return pl.pallas_call(
        _in_proj_kernel,
        grid=(m // tm,),
        in_specs=[pl.BlockSpec((tm, d), row), _resident((1, d)), _resident((d, n)), _resident((1, n)),
                  pl.BlockSpec((tm, LANES), tab), pl.BlockSpec((tm, LANES), tab),
                  pl.BlockSpec((tm, LANES), tab)],
        out_specs=[pl.BlockSpec((PARTS, tm, KV_DIM), parts), pl.BlockSpec((tm, KV_DIM), row),
                   pl.BlockSpec((tm, KV_DIM), row), pl.BlockSpec((PARTS, tm, KV_DIM), parts),
                   pl.BlockSpec((1, WINDOW, KV_DIM), seq), pl.BlockSpec((1, WINDOW, KV_DIM), seq),
                   pl.BlockSpec((1, hist, conv_ch), seq)],
        out_shape=[jax.ShapeDtypeStruct((PARTS, m, KV_DIM), BF16), jax.ShapeDtypeStruct((m, KV_DIM), F32),
                   jax.ShapeDtypeStruct((m, KV_DIM), F32), jax.ShapeDtypeStruct((PARTS, m, KV_DIM), F32),
                   jax.ShapeDtypeStruct((batch, WINDOW, KV_DIM), F32),
                   jax.ShapeDtypeStruct((batch, WINDOW, KV_DIM), F32),
                   jax.ShapeDtypeStruct((batch, hist, conv_ch), F32)],
        compiler_params=_params("arbitrary"),
        name="in_proj",
    )(x, g, w, b, cos, sup, sdn)


def _attn_scores(q, k_prev, k_own, v_prev, v_own, has_prev):
    blk = q.shape[0]
    kk = jnp.concatenate([k_prev, k_own], axis=0).astype(BF16)
    vv = jnp.concatenate([v_prev, v_own], axis=0).astype(BF16)
    lane_group = lax.broadcasted_iota(jnp.int32, kk.shape, 1) // HEAD_DIM
    zero = jnp.zeros_like(kk)
    kstack = jnp.concatenate([jnp.where(lane_group == g, kk, zero) for g in range(N_KV_HEADS)], axis=0)
    vstack = jnp.concatenate([jnp.where(lane_group == g, vv, zero) for g in range(N_KV_HEADS)], axis=0)
    qi = lax.broadcasted_iota(jnp.int32, (blk, 2 * blk), 0)
    kj = lax.broadcasted_iota(jnp.int32, (blk, 2 * blk), 1)
    rel = qi + blk - kj
    valid = (rel >= 0) & (rel < WINDOW) & (has_prev | (kj >= blk))
    return _dot_nt(q, kstack), valid, vstack


def _attn_probs(s_all, valid, sink_ref, part, zero_tile):
    width = valid.shape[1]
    probs = []
    for g in range(N_KV_HEADS):
        s = jnp.where(valid, s_all[:, g * width:(g + 1) * width], -jnp.inf)
        p = _sink_softmax(s, sink_ref[g * GROUP + part])
        if g == 0:
            top = jnp.concatenate([p[0:SUBLANES, 0:LANES] + zero_tile, p[0:SUBLANES, LANES:]], axis=1)
            p = jnp.concatenate([top, p[SUBLANES:, :]], axis=0)
        probs.append(p.astype(BF16))
    return jnp.concatenate(probs, axis=1)


def _conv_stage(u_own, halo, ue_ref, has_prev):
    ue_ref[0:CONV_HALO, :] = jnp.where(has_prev, halo, jnp.zeros_like(halo))
    ue_ref[CONV_HALO:, :] = u_own


def _conv_lanes(c, cw_ref, cb_ref, ue_ref, c_out):
    blk = ue_ref.shape[-2] - CONV_HALO
    first = CONV_HALO - (CONV_W - 1)
    cs = slice(c * LANES, (c + 1) * LANES)
    seen = None
    for s0 in range(0, blk, CONV_ROWS):
        out = cb_ref[:, cs]
        for r in range(SUBLANES):
            rows = CONV_ROWS + (SUBLANES if r else 0)
            part = None
            for a in range((first + CONV_W - 1 - r) // SUBLANES + 1):
                t = a * SUBLANES + r - first
                if 0 <= t < CONV_W:
                    lo = s0 + a * SUBLANES
                    term = cw_ref[t:t + 1, cs] * ue_ref[lo:lo + rows, cs]
                    part = term if part is None else part + term
            out = out + part[r:r + CONV_ROWS, :]
        c_out[s0:s0 + CONV_ROWS, cs] = out
        for r0 in range(0, CONV_ROWS, SUBLANES):
            tile = out[r0:r0 + SUBLANES, :]
            seen = tile if seen is None else jnp.maximum(seen, tile)
    return seen


def _mlp_mix_kernel(x_ref, q_ref, k_ref, kp_ref, v_ref, vp_ref, u_ref, uh_ref, sink_ref, cw_ref, cb_ref,
                    lg_ref, lb_ref, wo_ref, bo_ref, g2_ref, gf_ref, wu_ref, wd_ref,
                    o_ref, hn_ref, a_s, c_s, ue_ref, *, nb, nrow, final_norm):
    i, f = pl.program_id(0), pl.program_id(1)
    blk = WINDOW
    blocks_per_tile = x_ref.shape[0] // blk
    parts_per_step = ue_ref.shape[0]
    steps_per_block = PARTS // parts_per_step

    @pl.when((i == 0) & (f == 0))
    def _():
        a_s[...] = jnp.zeros_like(a_s)
        c_s[...] = jnp.zeros_like(c_s)

    @pl.when(f == 0)
    def _():
        a = jnp.concatenate([a_s[p] for p in range(PARTS)], axis=1)
        conv = jnp.concatenate([c_s[p] for p in range(PARTS)], axis=1)
        c = _layer_norm_swish(conv, lg_ref[...], lb_ref[...]).astype(BF16)
        mix = [ma + mc for ma, mc in zip(_dot_chunks(a, wo_ref, slice(0, Q_DIM)),
                                         _dot_chunks(c, wo_ref, slice(Q_DIM, None)))]
        h = x_ref[...] + jnp.concatenate(mix, axis=1) + bo_ref[...]
        o_ref[...] = h
        hn_ref[...] = _rms_norm_f32(h, g2_ref[...]).astype(hn_ref.dtype)

    sub, part0 = f // steps_per_block, (f % steps_per_block) * parts_per_step
    parts = [part0 + j for j in range(parts_per_step)]
    gblk = jnp.minimum(i, nrow - 1) * blocks_per_tile + sub
    has_prev = gblk % nb != 0
    rows = pl.ds(pl.multiple_of(sub * blk, blk), blk)
    prev_rows = pl.ds(pl.multiple_of(jnp.maximum(sub - 1, 0) * blk, blk), blk)
    halo_rows = pl.ds(pl.multiple_of(jnp.maximum(sub * blk - CONV_HALO, 0), CONV_HALO), CONV_HALO)
    k_prev = jnp.where(sub == 0, kp_ref[...], k_ref[prev_rows, :])
    v_prev = jnp.where(sub == 0, vp_ref[...], v_ref[prev_rows, :])
    conv_done = []
    for j, part in enumerate(parts):
        halo = jnp.where(sub == 0, uh_ref[part], u_ref[part, halo_rows, :])
        ue, conv_out = ue_ref.at[j], c_s.at[part, rows, :]
        _conv_stage(u_ref[part, rows, :], halo, ue, has_prev)
        seen = jnp.maximum(_conv_lanes(0, cw_ref.at[part], cb_ref.at[part], ue, conv_out),
                           _conv_lanes(1, cw_ref.at[part], cb_ref.at[part], ue, conv_out))
        conv_done.append(jnp.minimum(jnp.abs(seen), 0.0))
    scores = [_attn_scores(q_ref[part, rows, :], k_prev, k_ref[rows, :], v_prev, v_ref[rows, :], has_prev)
              for part in parts]
    up = jnp.concatenate(_dot_chunks(hn_ref[...], wu_ref), axis=1)
    act = jnp.square(jnp.maximum(up, 0.0)).astype(BF16)
    for part, (s_all, valid, vstack), zero in zip(parts, scores, conv_done):
        probs = _attn_probs(s_all, valid, sink_ref, part, zero)
        a_s[part, rows, :] = _dot(probs, vstack).astype(a_s.dtype)
    for c, down in enumerate(_dot_chunks(act, wd_ref)):
        o_ref[:, c * COL_CHUNK:(c + 1) * COL_CHUNK] += down

    if final_norm:
        @pl.when(f == pl.num_programs(1) - 1)
        def _():
            o_ref[...] = _rms_norm_f32(o_ref[...], gf_ref[...])


def _mlp_mix(x, q, k, v, u, sinks, conv_w, conv_b, ln_g, ln_b, w_out, b_out, g2, gf, w_up, w_down,
             seq, tm, parts_per_step, final_norm):
    m, d = x.shape
    conv_ch = u.shape[0] * u.shape[2]
    ff = w_down.shape[1]
    blk = WINDOW
    nb = seq // blk
    nrow = m // tm
    bpt = tm // blk
    nf = bpt * PARTS // parts_per_step
    assert ff % (nf * COL_CHUNK) == 0 and conv_ch == PARTS * KV_DIM and Q_DIM == PARTS * KV_DIM
    tf = ff // nf
    hpb = blk // CONV_HALO
    mix_row = lambda i: jnp.minimum(i, nrow - 1)
    mlp_tile = lambda i, f: (jnp.maximum(i - 1, 0), 0)
    mix_tile = lambda i, f: (mix_row(i), 0)
    mix_parts = lambda i, f: (0, mix_row(i), 0)
    prev = lambda i, f: (jnp.maximum(mix_row(i) * bpt - 1, 0), 0)
    halo_parts = lambda i, f: (0, jnp.maximum(mix_row(i) * bpt * hpb - 1, 0), 0)
    return pl.pallas_call(
        functools.partial(_mlp_mix_kernel, nb=nb, nrow=nrow, final_norm=final_norm),
        grid=(nrow + 1, nf),
        in_specs=[pl.BlockSpec((tm, d), mlp_tile),
                  pl.BlockSpec((PARTS, tm, KV_DIM), mix_parts),
                  pl.BlockSpec((tm, KV_DIM), mix_tile), pl.BlockSpec((blk, KV_DIM), prev),
                  pl.BlockSpec((tm, KV_DIM), mix_tile), pl.BlockSpec((blk, KV_DIM), prev),
                  pl.BlockSpec((PARTS, tm, KV_DIM), mix_parts),
                  pl.BlockSpec((PARTS, CONV_HALO, KV_DIM), halo_parts),
                  pl.BlockSpec(memory_space=pltpu.SMEM),
                  _resident((PARTS, CONV_W, KV_DIM)), _resident((PARTS, 1, KV_DIM)),
                  _resident((1, conv_ch)), _resident((1, conv_ch)),
                  _resident(w_out.shape), _resident((1, d)), _resident((1, d)), _resident((1, d)),
                  pl.BlockSpec((tf // COL_CHUNK, d, COL_CHUNK), lambda i, f: (f, 0, 0)),
                  pl.BlockSpec((w_down.shape[0], tf, COL_CHUNK), lambda i, f: (0, f, 0))],
        out_specs=pl.BlockSpec((tm, d), mlp_tile),
        out_shape=jax.ShapeDtypeStruct((m, d), F32),
        scratch_shapes=[pltpu.VMEM((tm, d), BF16),
                        pltpu.VMEM((PARTS, tm, KV_DIM), BF16),
                        pltpu.VMEM((PARTS, tm, KV_DIM), F32),
                        pltpu.VMEM((parts_per_step, CONV_HALO + blk, KV_DIM), F32)],
        compiler_params=_params("arbitrary", "arbitrary"),
        name="mlp_mix",
    )(x, q, k, k, v, v, u, u, sinks, conv_w, conv_b, ln_g, ln_b, w_out, b_out, g2, gf, w_up, w_down)


def _in_proj_sample_kernel(x_ref, g_ref, wq_ref, bq_ref, wkv_ref, bkv_ref, wu_ref, bu_ref,
                           cos_ref, sup_ref, sdn_ref, cosc_ref, supc_ref, sdnc_ref,
                           q_ref, kt_ref, vt_ref, u_ref):
    conv_ch = u_ref.shape[1]
    hn = _rms_norm_f32(x_ref[...], g_ref[...]).astype(BF16)
    zq = _dot(hn, wq_ref[...]) + bq_ref[...]
    q_ref[...] = _rope_lanes(zq, cos_ref[...], sup_ref[...], sdn_ref[...]) * ATTN_SCALE
    zkv = _dot_nt(wkv_ref[...], hn) + bkv_ref[...]
    kt_ref[...] = _rope(zkv[0:KV_DIM, :], cosc_ref[...], supc_ref[...], sdnc_ref[...], 0)
    vt_ref[...] = zkv[KV_DIM:, :]
    zu = _dot(hn, wu_ref[:, 0:conv_ch]) + bu_ref[:, 0:conv_ch]
    zg = _dot(hn, wu_ref[:, conv_ch:]) + bu_ref[:, conv_ch:]
    u_ref[...] = zu * jax.nn.sigmoid(zg)


def _in_proj_sample(x, g, wq, bq, wkv_t, bkv_c, wu, bu, row_tabs, col_tabs):
    n, d = x.shape
    conv_ch = wu.shape[1] // 2
    full = lambda a: _resident(a.shape)
    args = (x, g, wq, bq, wkv_t, bkv_c, wu, bu, *row_tabs, *col_tabs)
    return pl.pallas_call(
        _in_proj_sample_kernel,
        grid=(1,),
        in_specs=[full(a) for a in args],
        out_specs=[_resident((n, Q_DIM)), _resident((KV_DIM, n)), _resident((KV_DIM, n)), _resident((n, conv_ch))],
        out_shape=[jax.ShapeDtypeStruct((n, Q_DIM), F32), jax.ShapeDtypeStruct((KV_DIM, n), F32),
                   jax.ShapeDtypeStruct((KV_DIM, n), F32), jax.ShapeDtypeStruct((n, conv_ch), F32)],
        compiler_params=_params("arbitrary"),
        name="in_proj_sample",
    )(*args)


def _attn_sample_kernel(q_ref, kn_ref, vn_ref, ck_ref, cv_ref, sink_ref, ao_ref, nk_ref, nv_ref):
    nseq, _, wb = ck_ref.shape
    rows = N_HEADS
    lane_group = lax.broadcasted_iota(jnp.int32, (rows, KV_DIM), 1) // HEAD_DIM
    row_group = lax.broadcasted_iota(jnp.int32, (rows, KV_DIM), 0) % N_KV_HEADS
    own_lanes = lane_group == row_group
    fold = (lax.broadcasted_iota(jnp.int32, (SUBLANES, rows), 1) // N_KV_HEADS
            == lax.broadcasted_iota(jnp.int32, (SUBLANES, rows), 0)).astype(F32)
    newest = lax.broadcasted_iota(jnp.int32, (KV_DIM, wb), 1) == wb - 1
    sink = sink_ref[...]
    kn, vn = kn_ref[0], vn_ref[0]

    seqs = range(nseq)
    knew = [jnp.where(newest, kn[:, n:n + 1], pltpu.roll(ck_ref[n], wb - 1, 1)) for n in seqs]
    vnew = [jnp.where(newest, vn[:, n:n + 1], pltpu.roll(cv_ref[n], wb - 1, 1)) for n in seqs]
    for n in seqs:
        nk_ref[n] = knew[n]
        nv_ref[n] = vnew[n]
    qrows = [jnp.where(own_lanes, q_ref[n], jnp.zeros((rows, KV_DIM), F32)).astype(BF16) for n in seqs]
    s = [_dot(qrows[n], knew[n].astype(BF16)) for n in seqs]
    p = [_sink_softmax(s[n], sink).astype(BF16) for n in seqs]
    o = [_dot_nt(p[n], vnew[n].astype(BF16)) for n in seqs]
    for n in seqs:
        ao_ref[n] = _dot(fold, jnp.where(own_lanes, o[n], jnp.zeros_like(o[n])))


def _attn_sample(q_rep, kn, vn, cache_kt, cache_vt, sink_rows, nseq):
    n, _, wb = cache_kt.shape
    r3 = lambda s: (s, 0, 0)
    return pl.pallas_call(
        _attn_sample_kernel,
        grid=(n // nseq,),
        in_specs=[pl.BlockSpec((nseq, N_HEADS, KV_DIM), r3),
                  pl.BlockSpec((1, KV_DIM, nseq), r3), pl.BlockSpec((1, KV_DIM, nseq), r3),
                  pl.BlockSpec((nseq, KV_DIM, wb), r3), pl.BlockSpec((nseq, KV_DIM, wb), r3),
                  _resident((N_HEADS, 1))],
        out_specs=[pl.BlockSpec((nseq, SUBLANES, KV_DIM), r3),
                   pl.BlockSpec((nseq, KV_DIM, wb), r3), pl.BlockSpec((nseq, KV_DIM, wb), r3)],
        out_shape=[jax.ShapeDtypeStruct((n, SUBLANES, KV_DIM), F32),
                   jax.ShapeDtypeStruct((n, KV_DIM, wb), F32), jax.ShapeDtypeStruct((n, KV_DIM, wb), F32)],
        compiler_params=_params("arbitrary"),
        name="attn_sample",
    )(q_rep, kn, vn, cache_kt, cache_vt, sink_rows)


def _conv_sample_kernel(cc_ref, cn_ref, u_ref, cw_ref, cb_ref, lg_ref, lb_ref, co_ref, nc_ref, acc_ref, *, hist):
    j = pl.program_id(0)
    u = u_ref[...]

    @pl.when(j == 0)
    def _():
        acc_ref[...] = jnp.zeros_like(acc_ref)

    acc_ref[...] += cw_ref[j] * cc_ref[0]

    @pl.when(j < hist - 1)
    def _():
        nc_ref[0] = cn_ref[0]

    @pl.when(j == hist - 1)
    def _():
        nc_ref[0] = u
        y = acc_ref[...] + cw_ref[hist] * u + cb_ref[...]
        co_ref[...] = _layer_norm_swish(y, lg_ref[...], lb_ref[...]).astype(co_ref.dtype)


def _conv_sample(cache_t, u, conv_w, conv_b, ln_g, ln_b):
    hist, n, ch = cache_t.shape
    tap = lambda j: (j, 0, 0)
    return pl.pallas_call(
        functools.partial(_conv_sample_kernel, hist=hist),
        grid=(hist,),
        in_specs=[pl.BlockSpec((1, n, ch), tap),
                  pl.BlockSpec((1, n, ch), lambda j: (jnp.minimum(j + 1, hist - 1), 0, 0)),
                  _resident((n, ch)), _resident((hist + 1, 1, ch)),
                  _resident((1, ch)), _resident((1, ch)), _resident((1, ch))],
        out_specs=[_resident((n, ch)), pl.BlockSpec((1, n, ch), tap)],
        out_shape=[jax.ShapeDtypeStruct((n, ch), BF16), jax.ShapeDtypeStruct((hist, n, ch), F32)],
        scratch_shapes=[pltpu.VMEM((n, ch), F32)],
        compiler_params=_params("arbitrary"),
        name="conv_sample",
    )(cache_t, cache_t, u, conv_w.reshape(hist + 1, 1, ch), conv_b, ln_g, ln_b)


def _out_mlp_kernel(x_ref, a_ref, c_ref, wo_ref, bo_ref, g2_ref, wu_ref, wd_ref, gf_ref,
                    o_ref, hn_ref, *, final_norm):
    f = pl.program_id(1)

    @pl.when(f == 0)
    def _():
        mix = [ma + mc for ma, mc in zip(_dot_chunks(a_ref[...], wo_ref, slice(0, Q_DIM)),
                                         _dot_chunks(c_ref[...], wo_ref, slice(Q_DIM, None)))]
        h = x_ref[...] + jnp.concatenate(mix, axis=1) + bo_ref[...]
        o_ref[...] = h
        hn_ref[...] = _rms_norm_f32(h, g2_ref[...]).astype(hn_ref.dtype)

    up = jnp.concatenate(_dot_chunks(hn_ref[...], wu_ref), axis=1)
    act = jnp.square(jnp.maximum(up, 0.0)).astype(BF16)
    for c, down in enumerate(_dot_chunks(act, wd_ref)):
        o_ref[:, c * COL_CHUNK:(c + 1) * COL_CHUNK] += down

    if final_norm:
        @pl.when(f == pl.num_programs(1) - 1)
        def _():
            o_ref[...] = _rms_norm_f32(o_ref[...], gf_ref[...])


def _out_mlp(x, a, c, w_out, b_out, g2, w_up, w_down, gf, tm, tf, final_norm):
    m, d = x.shape
    conv_ch = w_out.shape[1] - Q_DIM
    ff = w_down.shape[1]
    row = lambda i, f: (i, 0)
    return pl.pallas_call(
        functools.partial(_out_mlp_kernel, final_norm=final_norm),
        grid=(m // tm, ff // tf),
        in_specs=[pl.BlockSpec((tm, d), row), pl.BlockSpec((tm, Q_DIM), row),
                  pl.BlockSpec((tm, conv_ch), row),
                  _resident(w_out.shape), _resident((1, d)), _resident((1, d)),
                  pl.BlockSpec((tf // COL_CHUNK, d, COL_CHUNK), lambda i, f: (f, 0, 0)),
                  pl.BlockSpec((w_down.shape[0], tf, COL_CHUNK), lambda i, f: (0, f, 0)),
                  _resident((1, d))],
        out_specs=pl.BlockSpec((tm, d), row),
        out_shape=jax.ShapeDtypeStruct((m, d), F32),
        scratch_shapes=[pltpu.VMEM((tm, d), BF16)],
        compiler_params=_params("arbitrary", "arbitrary"),
        name="out_mlp",
    )(x, a, c, w_out, b_out, g2, w_up, w_down, gf)


def _rope_tables(pos):
    half = ROT_DIM // 2
    inv_freq = jnp.power(jnp.float32(ROPE_THETA), -jnp.arange(half, dtype=jnp.float32) * 2.0 / ROT_DIM)
    ang = pos.astype(jnp.float32)[:, None] * inv_freq[None, :]
    cos, sin = jnp.cos(ang), jnp.sin(ang)
    t = pos.shape[0]
    pad = jnp.zeros((t, HEAD_DIM - ROT_DIM), F32)
    zeros = jnp.zeros((t, half), F32)
    per_head = lambda a, b, fill: jnp.concatenate([a, b, pad + fill], axis=1)
    reps = LANES // HEAD_DIM
    cos_t = jnp.tile(per_head(cos, cos, 1.0), (1, reps))
    sup_t = jnp.tile(per_head(zeros, sin, 0.0), (1, reps))
    sdn_t = jnp.tile(per_head(-sin, zeros, 0.0), (1, reps))
    return cos_t, sup_t, sdn_t


def _heads_to_igd(a, axis):
    shape = a.shape
    a = a.reshape(shape[:axis] + (N_KV_HEADS, GROUP, HEAD_DIM) + shape[axis + 1:])
    return jnp.swapaxes(a, axis, axis + 1).reshape(shape)


def _cast_slabs_kernel(w_ref, o_ref):
    o_ref[0] = w_ref[...].astype(o_ref.dtype)


def _cast_slabs(w, tk):
    k, n = w.shape
    return pl.pallas_call(
        _cast_slabs_kernel,
        grid=(n // COL_CHUNK, k // tk),
        in_specs=[pl.BlockSpec((tk, COL_CHUNK), lambda j, kk: (kk, j))],
        out_specs=pl.BlockSpec((1, tk, COL_CHUNK), lambda j, kk: (j, kk, 0)),
        out_shape=jax.ShapeDtypeStruct((n // COL_CHUNK, k, COL_CHUNK), BF16),
        compiler_params=_params("arbitrary", "arbitrary"),
        name="cast_slabs",
    )(w)


def _col_chunks(w):
    k, n = w.shape
    return w.reshape(k, n // COL_CHUNK, COL_CHUNK).transpose(1, 0, 2)


def _tile(m, target):
    t = min(m, target)
    assert m % t == 0, (m, t)
    return t


def kernel(x_prompt, x_sample, cache_k, cache_v, cache_conv, norm1_g, w_in, b_in, attn_sinks,
           conv_w, conv_b, conv_ln_g, conv_ln_b, w_out, b_out, norm2_g, w_up, w_down, final_norm_g):
    batch, seq, d_model = x_prompt.shape
    nsamp, t_s, _ = x_sample.shape
    depth = w_in.shape[0]
    conv_ch = conv_w.shape[2]
    wb = cache_k.shape[2]
    hist = CONV_W - 1
    assert t_s == 1 and seq % WINDOW == 0 and wb == WINDOW and cache_conv.shape[2] == hist
    assert depth >= 1

    tabs_p = _rope_tables(jnp.arange(seq, dtype=jnp.int32))
    tabs_s1 = _rope_tables(PAST_LEN + jnp.arange(t_s, dtype=jnp.int32))
    tabs_s = tuple(jnp.broadcast_to(t, (nsamp, LANES)) for t in tabs_s1)
    tabs_sc = tuple(jnp.tile(t[0], KV_DIM // LANES).reshape(KV_DIM, 1) for t in tabs_s1)
    row = lambda a: a.reshape(1, -1)

    hp = x_prompt.reshape(batch * seq, d_model)
    hs = x_sample.reshape(nsamp * t_s, d_model)
    tm_in = _tile(seq, 512)
    tm_mlp = _tile(batch * seq, 512)
    tf_s = 2 * COL_CHUNK
    nseq = _tile(nsamp, 8)
    o1, o2 = Q_DIM, Q_DIM + 2 * KV_DIM
    pk, pv, pc, sk, sv, sc = [], [], [], [], [], []
    for l in range(depth):
        last = l == depth - 1
        wq = _heads_to_igd(w_in[l][:, :o1], 1).astype(BF16)
        bq = _heads_to_igd(b_in[l][:o1], 0)
        w_rest = w_in[l][:, o1:].astype(BF16)
        w_in_l = jnp.concatenate([wq, w_rest], axis=1)
        b_in_l = row(jnp.concatenate([bq, b_in[l][o1:]]))
        w_out_l = _col_chunks(
            jnp.concatenate([_heads_to_igd(w_out[l][:Q_DIM], 0), w_out[l][Q_DIM:]], axis=0).astype(BF16))
        w_up_l, w_down_l = _cast_slabs(w_up[l], _tile(d_model, 2048)), _cast_slabs(w_down[l], 2048)
        sinks = attn_sinks[l].astype(F32)
        sink_rows = sinks.reshape(N_KV_HEADS, GROUP).T.reshape(N_HEADS, 1)
        g1, g2, gf = row(norm1_g[l]), row(norm2_g[l]), row(final_norm_g)
        cb, lg, lb, bo = row(conv_b[l]), row(conv_ln_g[l]), row(conv_ln_b[l]), row(b_out[l])

        q, k, v, u, pk_l, pv_l, pc_l = _in_proj(hp, g1, w_in_l, b_in_l, *tabs_p, tm_in, batch)
        by_part = lambda a: a.reshape(a.shape[0], PARTS, KV_DIM).transpose(1, 0, 2)
        hp = _mlp_mix(hp, q, k, v, u, sinks, by_part(conv_w[l]), by_part(cb), lg, lb, w_out_l, bo, g2, gf,
                      w_up_l, w_down_l, seq, tm_mlp, 1, last)
        pk.append(pk_l.reshape(batch, WINDOW, N_KV_HEADS, HEAD_DIM))
        pv.append(pv_l.reshape(batch, WINDOW, N_KV_HEADS, HEAD_DIM))
        pc.append(pc_l)

        cache_kt = jnp.transpose(cache_k[l], (0, 2, 3, 1)).reshape(nsamp, KV_DIM, wb)
        cache_vt = jnp.transpose(cache_v[l], (0, 2, 3, 1)).reshape(nsamp, KV_DIM, wb)
        cache_ct = jnp.transpose(cache_conv[l], (1, 0, 2))
        wkv_t = w_in[l][:, o1:o2].T.astype(BF16)
        bkv_c = b_in[l][o1:o2].reshape(2 * KV_DIM, 1)
        q, kt, vt, u = _in_proj_sample(hs, g1, wq, row(bq), wkv_t, bkv_c, w_rest[:, 2 * KV_DIM:],
                                       row(b_in[l][o2:]), tabs_s, tabs_sc)
        q_rep = jnp.repeat(q.reshape(nsamp, GROUP, KV_DIM), N_KV_HEADS, axis=1)
        by_block = lambda a: a.reshape(KV_DIM, nsamp // nseq, nseq).transpose(1, 0, 2)
        a8, nkt, nvt = _attn_sample(q_rep, by_block(kt), by_block(vt), cache_kt, cache_vt, sink_rows, nseq)
        c_o, nct = _conv_sample(cache_ct, u, conv_w[l], cb, lg, lb)
        a_o = a8[:, :GROUP, :].reshape(nsamp, Q_DIM).astype(BF16)
        hs = _out_mlp(hs, a_o, c_o, w_out_l, bo, g2, w_up_l, w_down_l, gf, nsamp, tf_s, last)
        to_cache = lambda a: jnp.transpose(a.reshape(nsamp, N_KV_HEADS, HEAD_DIM, wb), (0, 3, 1, 2))
        sk.append(to_cache(nkt))
        sv.append(to_cache(nvt))
        sc.append(jnp.transpose(nct, (1, 0, 2)))

    y_prompt = hp.reshape(batch, seq, d_model)
    y_sample = hs.reshape(nsamp, t_s, d_model)
    return (y_prompt, y_sample, jnp.stack(pk), jnp.stack(pv), jnp.stack(pc),
            jnp.stack(sk), jnp.stack(sv), jnp.stack(sc))
```

```python
import functools

import jax
import jax.numpy as jnp
import numpy as np
from jax import lax
from jax.experimental import pallas as pl
from jax.experimental.pallas import tpu as pltpu

HEAD_DIM = 64
N_HEADS = 16
N_KV_HEADS = 4
GROUP = N_HEADS // N_KV_HEADS
WINDOW = 128
ROT_DIM = HEAD_DIM // 4
ROPE_THETA = 500000.0
ATTN_SCALE = HEAD_DIM ** -0.5
Q_DIM = N_HEADS * HEAD_DIM
KV_DIM = N_KV_HEADS * HEAD_DIM
CONV_W = 31
PAST_LEN = 16384
EPS = 1e-5

LANES = 128
SUBLANES = 8
CONV_HALO = 32
CONV_ROWS = 64
VMEM_LIMIT_BYTES = 60 * 1024 * 1024
PARTS = GROUP
COL_CHUNK = 512

BF16 = jnp.bfloat16
F32 = jnp.float32


def _resident(shape):
    return pl.BlockSpec(shape, lambda *_: (0,) * len(shape), pipeline_mode=pl.Buffered(1))


def _params(*semantics):
    return pltpu.CompilerParams(dimension_semantics=semantics, vmem_limit_bytes=VMEM_LIMIT_BYTES)


def _rms_norm_f32(x, g):
    return x * lax.rsqrt(jnp.mean(x * x, axis=-1, keepdims=True) + EPS) * g


def _layer_norm_swish(y, g, b):
    mu = jnp.mean(y, axis=-1, keepdims=True)
    yc = y - mu
    var = jnp.mean(yc * yc, axis=-1, keepdims=True)
    yn = yc * lax.rsqrt(var + EPS) * g + b
    return yn * jax.nn.sigmoid(yn)


def _dot(a, b):
    return jnp.dot(a, b, preferred_element_type=F32)


def _dot_nt(a, b):
    return lax.dot_general(a, b, (((1,), (1,)), ((), ())), preferred_element_type=F32)


def _dot_chunks(a, w_ref, rows=slice(None)):
    return [_dot(a, w_ref[c, rows, :]) for c in range(w_ref.shape[0])]


def _sink_softmax(s, sink):
    m = jnp.maximum(jnp.max(s, axis=-1, keepdims=True), sink)
    p = jnp.exp(s - m)
    return p / (jnp.sum(p, axis=-1, keepdims=True) + jnp.exp(sink - m))


def _rope(z, cos, sin_up, sin_dn, axis):
    n = z.shape[axis]
    half = ROT_DIM // 2
    return z * cos + pltpu.roll(z, half, axis) * sin_up + pltpu.roll(z, n - half, axis) * sin_dn


def _rope_lanes(z, cos, sin_up, sin_dn):
    cols = [_rope(z[:, c * LANES:(c + 1) * LANES], cos, sin_up, sin_dn, 1) for c in range(z.shape[1] // LANES)]
    return jnp.concatenate(cols, axis=1)


def _in_proj_kernel(x_ref, g_ref, w_ref, b_ref, cos_ref, sup_ref, sdn_ref,
                    q_ref, k_ref, v_ref, u_ref, pk_ref, pv_ref, pc_ref):
    conv_ch = u_ref.shape[0] * u_ref.shape[2]
    tm = x_ref.shape[0]
    hist = pc_ref.shape[1]
    o1, o2, o3, o4 = Q_DIM, Q_DIM + KV_DIM, Q_DIM + 2 * KV_DIM, Q_DIM + 2 * KV_DIM + conv_ch
    hn = _rms_norm_f32(x_ref[...], g_ref[...]).astype(BF16)
    cos, sup, sdn = cos_ref[...], sup_ref[...], sdn_ref[...]

    zq = _dot(hn, w_ref[:, 0:o1]) + b_ref[:, 0:o1]
    q = (_rope_lanes(zq, cos, sup, sdn) * ATTN_SCALE).astype(q_ref.dtype)
    zk = _dot(hn, w_ref[:, o1:o2]) + b_ref[:, o1:o2]
    k_ref[...] = _rope_lanes(zk, cos, sup, sdn)
    v_ref[...] = _dot(hn, w_ref[:, o2:o3]) + b_ref[:, o2:o3]
    zu = _dot(hn, w_ref[:, o3:o4]) + b_ref[:, o3:o4]
    zg = _dot(hn, w_ref[:, o4:]) + b_ref[:, o4:]
    u = zu * jax.nn.sigmoid(zg)
    for p in range(PARTS):
        q_ref[p] = q[:, p * KV_DIM:(p + 1) * KV_DIM]
        u_ref[p] = u[:, p * KV_DIM:(p + 1) * KV_DIM]
    pk_ref[0] = k_ref[tm - WINDOW:, :]
    pv_ref[0] = v_ref[tm - WINDOW:, :]
    pc_ref[0] = u[tm - hist:, :]


def _in_proj(x, g, w, b, cos, sup, sdn, tm, batch):
    m, d = x.shape
    n = w.shape[1]
    conv_ch = (n - Q_DIM - 2 * KV_DIM) // 2
    nt = cos.shape[0] // tm
    hist = CONV_W - 1
    row = lambda i: (i, 0)
    parts = lambda i: (0, i, 0)
    tab = lambda i: (i % nt, 0)
    seq = lambda i: (i // nt, 0, 0)
    assert conv_ch == PARTS * KV_DIM and Q_DIM == PARTS * KV_DIM
    return pl.pallas_call(
        _in_proj_kernel,
        grid=(m // tm,),
        in_specs=[pl.BlockSpec((tm, d), row), _resident((1, d)), _resident((d, n)), _resident((1, n)),
                  pl.BlockSpec((tm, LANES), tab), pl.BlockSpec((tm, LANES), tab),
                  pl.BlockSpec((tm, LANES), tab)],
        out_specs=[pl.BlockSpec((PARTS, tm, KV_DIM), parts), pl.BlockSpec((tm, KV_DIM), row),
                   pl.BlockSpec((tm, KV_DIM), row), pl.BlockSpec((PARTS, tm, KV_DIM), parts),
                   pl.BlockSpec((1, WINDOW, KV_DIM), seq), pl.BlockSpec((1, WINDOW, KV_DIM), seq),
                   pl.BlockSpec((1, hist, conv_ch), seq)],
        out_shape=[jax.ShapeDtypeStruct((PARTS, m, KV_DIM), BF16), jax.ShapeDtypeStruct((m, KV_DIM), F32),
                   jax.ShapeDtypeStruct((m, KV_DIM), F32), jax.ShapeDtypeStruct((PARTS, m, KV_DIM), F32),
                   jax.ShapeDtypeStruct((batch, WINDOW, KV_DIM), F32),
                   jax.ShapeDtypeStruct((batch, WINDOW, KV_DIM), F32),
                   jax.ShapeDtypeStruct((batch, hist, conv_ch), F32)],
        compiler_params=_params("arbitrary"),
        name="in_proj",
    )(x, g, w, b, cos, sup, sdn)


def _attn_scores(q, k_prev, k_own, v_prev, v_own, has_prev):
    blk = q.shape[0]
    kk = jnp.concatenate([k_prev, k_own], axis=0).astype(BF16)
    vv = jnp.concatenate([v_prev, v_own], axis=0).astype(BF16)
    lane_group = lax.broadcasted_iota(jnp.int32, kk.shape, 1) // HEAD_DIM
    zero = jnp.zeros_like(kk)
    kstack = jnp.concatenate([jnp.where(lane_group == g, kk, zero) for g in range(N_KV_HEADS)], axis=0)
    vstack = jnp.concatenate([jnp.where(lane_group == g, vv, zero) for g in range(N_KV_HEADS)], axis=0)
    qi = lax.broadcasted_iota(jnp.int32, (blk, 2 * blk), 0)
    kj = lax.broadcasted_iota(jnp.int32, (blk, 2 * blk), 1)
    rel = qi + blk - kj
    valid = (rel >= 0) & (rel < WINDOW) & (has_prev | (kj >= blk))
    return _dot_nt(q, kstack), valid, vstack


def _attn_probs(s_all, valid, sink_ref, part, zero_tile):
    width = valid.shape[1]
    probs = []
    for g in range(N_KV_HEADS):
        s = jnp.where(valid, s_all[:, g * width:(g + 1) * width], -jnp.inf)
        p = _sink_softmax(s, sink_ref[g * GROUP + part])
        if g == 0:
            top = jnp.concatenate([p[0:SUBLANES, 0:LANES] + zero_tile, p[0:SUBLANES, LANES:]], axis=1)
            p = jnp.concatenate([top, p[SUBLANES:, :]], axis=0)
        probs.append(p.astype(BF16))
    return jnp.concatenate(probs, axis=1)


def _conv_stage(u_own, halo, ue_ref, has_prev):
    ue_ref[0:CONV_HALO, :] = jnp.where(has_prev, halo, jnp.zeros_like(halo))
    ue_ref[CONV_HALO:, :] = u_own


def _conv_lanes(c, cw_ref, cb_ref, ue_ref, c_out):
    blk = ue_ref.shape[-2] - CONV_HALO
    first = CONV_HALO - (CONV_W - 1)
    cs = slice(c * LANES, (c + 1) * LANES)
    seen = None
    for s0 in range(0, blk, CONV_ROWS):
        win = ue_ref[s0:s0 + CONV_ROWS + CONV_HALO, cs]
        out = cb_ref[:, cs]
        for r in range(SUBLANES):
            rows = CONV_ROWS + (SUBLANES if r else 0)
            part = None
            for a in range((first + CONV_W - 1 - r) // SUBLANES + 1):
                t = a * SUBLANES + r - first
                if 0 <= t < CONV_W:
                    term = cw_ref[t:t + 1, cs] * win[a * SUBLANES:a * SUBLANES + rows, :]
                    part = term if part is None else part + term
            out = out + part[r:r + CONV_ROWS, :]
        c_out[s0:s0 + CONV_ROWS, cs] = out
        for r0 in range(0, CONV_ROWS, SUBLANES):
            tile = out[r0:r0 + SUBLANES, :]
            seen = tile if seen is None else jnp.maximum(seen, tile)
    return seen


def _mlp_mix_kernel(x_ref, q_ref, k_ref, kp_ref, v_ref, vp_ref, u_ref, uh_ref, sink_ref, cw_ref, cb_ref,
                    lg_ref, lb_ref, wo_ref, bo_ref, g2_ref, gf_ref, wu_ref, wd_ref,
                    o_ref, hn_ref, a_s, c_s, ue_ref, *, nb, nrow, final_norm):
    i, f = pl.program_id(0), pl.program_id(1)
    blk = WINDOW
    blocks_per_tile = x_ref.shape[0] // blk
    parts_per_step = ue_ref.shape[0]
    steps_per_block = PARTS // parts_per_step

    @pl.when((i == 0) & (f == 0))
    def _():
        a_s[...] = jnp.zeros_like(a_s)
        c_s[...] = jnp.zeros_like(c_s)

    @pl.when(f == 0)
    def _():
        a = jnp.concatenate([a_s[p] for p in range(PARTS)], axis=1)
        conv = jnp.concatenate([c_s[p] for p in range(PARTS)], axis=1)
        c = _layer_norm_swish(conv, lg_ref[...], lb_ref[...]).astype(BF16)
        mix = [ma + mc for ma, mc in zip(_dot_chunks(a, wo_ref, slice(0, Q_DIM)),
                                         _dot_chunks(c, wo_ref, slice(Q_DIM, None)))]
        h = x_ref[...] + jnp.concatenate(mix, axis=1) + bo_ref[...]
        o_ref[...] = h
        hn_ref[...] = _rms_norm_f32(h, g2_ref[...]).astype(hn_ref.dtype)

    sub, part0 = f // steps_per_block, (f % steps_per_block) * parts_per_step
    parts = [part0 + j for j in range(parts_per_step)]
    gblk = jnp.minimum(i, nrow - 1) * blocks_per_tile + sub
    has_prev = gblk % nb != 0
    rows = pl.ds(pl.multiple_of(sub * blk, blk), blk)
    prev_rows = pl.ds(pl.multiple_of(jnp.maximum(sub - 1, 0) * blk, blk), blk)
    halo_rows = pl.ds(pl.multiple_of(jnp.maximum(sub * blk - CONV_HALO, 0), CONV_HALO), CONV_HALO)
    k_prev = jnp.where(sub == 0, kp_ref[...], k_ref[prev_rows, :])
    v_prev = jnp.where(sub == 0, vp_ref[...], v_ref[prev_rows, :])
    conv_done = []
    for j, part in enumerate(parts):
        halo = jnp.where(sub == 0, uh_ref[part], u_ref[part, halo_rows, :])
        ue, conv_out = ue_ref.at[j], c_s.at[part, rows, :]
        _conv_stage(u_ref[part, rows, :], halo, ue, has_prev)
        seen = jnp.maximum(_conv_lanes(0, cw_ref.at[part], cb_ref.at[part], ue, conv_out),
                           _conv_lanes(1, cw_ref.at[part], cb_ref.at[part], ue, conv_out))
        conv_done.append(jnp.minimum(jnp.abs(seen), 0.0))
    scores = [_attn_scores(q_ref[part, rows, :], k_prev, k_ref[rows, :], v_prev, v_ref[rows, :], has_prev)
              for part in parts]
    up = jnp.concatenate(_dot_chunks(hn_ref[...], wu_ref), axis=1)
    act = jnp.square(jnp.maximum(up, 0.0)).astype(BF16)
    for part, (s_all, valid, vstack), zero in zip(parts, scores, conv_done):
        probs = _attn_probs(s_all, valid, sink_ref, part, zero)
        a_s[part, rows, :] = _dot(probs, vstack).astype(a_s.dtype)
    for c, down in enumerate(_dot_chunks(act, wd_ref)):
        o_ref[:, c * COL_CHUNK:(c + 1) * COL_CHUNK] += down

    if final_norm:
        @pl.when(f == pl.num_programs(1) - 1)
        def _():
            o_ref[...] = _rms_norm_f32(o_ref[...], gf_ref[...])


def _mlp_mix(x, q, k, v, u, sinks, conv_w, conv_b, ln_g, ln_b, w_out, b_out, g2, gf, w_up, w_down,
             seq, tm, parts_per_step, final_norm):
    m, d = x.shape
    conv_ch = u.shape[0] * u.shape[2]
    ff = w_down.shape[1]
    blk = WINDOW
    nb = seq // blk
    nrow = m // tm
    bpt = tm // blk
    nf = bpt * PARTS // parts_per_step
    assert ff % (nf * COL_CHUNK) == 0 and conv_ch == PARTS * KV_DIM and Q_DIM == PARTS * KV_DIM
    tf = ff // nf
    hpb = blk // CONV_HALO
    mix_row = lambda i: jnp.minimum(i, nrow - 1)
    mlp_tile = lambda i, f: (jnp.maximum(i - 1, 0), 0)
    mix_tile = lambda i, f: (mix_row(i), 0)
    mix_parts = lambda i, f: (0, mix_row(i), 0)
    prev = lambda i, f: (jnp.maximum(mix_row(i) * bpt - 1, 0), 0)
    halo_parts = lambda i, f: (0, jnp.maximum(mix_row(i) * bpt * hpb - 1, 0), 0)
    return pl.pallas_call(
        functools.partial(_mlp_mix_kernel, nb=nb, nrow=nrow, final_norm=final_norm),
        grid=(nrow + 1, nf),
        in_specs=[pl.BlockSpec((tm, d), mlp_tile),
                  pl.BlockSpec((PARTS, tm, KV_DIM), mix_parts),
                  pl.BlockSpec((tm, KV_DIM), mix_tile), pl.BlockSpec((blk, KV_DIM), prev),
                  pl.BlockSpec((tm, KV_DIM), mix_tile), pl.BlockSpec((blk, KV_DIM), prev),
                  pl.BlockSpec((PARTS, tm, KV_DIM), mix_parts),
                  pl.BlockSpec((PARTS, CONV_HALO, KV_DIM), halo_parts),
                  pl.BlockSpec(memory_space=pltpu.SMEM),
                  _resident((PARTS, CONV_W, KV_DIM)), _resident((PARTS, 1, KV_DIM)),
                  _resident((1, conv_ch)), _resident((1, conv_ch)),
                  _resident(w_out.shape), _resident((1, d)), _resident((1, d)), _resident((1, d)),
                  pl.BlockSpec((tf // COL_CHUNK, d, COL_CHUNK), lambda i, f: (f, 0, 0)),
                  pl.BlockSpec((w_down.shape[0], tf, COL_CHUNK), lambda i, f: (0, f, 0))],
        out_specs=pl.BlockSpec((tm, d), mlp_tile),
        out_shape=jax.ShapeDtypeStruct((m, d), F32),
        scratch_shapes=[pltpu.VMEM((tm, d), BF16),
                        pltpu.VMEM((PARTS, tm, KV_DIM), BF16),
                        pltpu.VMEM((PARTS, tm, KV_DIM), F32),
                        pltpu.VMEM((parts_per_step, CONV_HALO + blk, KV_DIM), F32)],
        compiler_params=_params("arbitrary", "arbitrary"),
        name="mlp_mix",
    )(x, q, k, k, v, v, u, u, sinks, conv_w, conv_b, ln_g, ln_b, w_out, b_out, g2, gf, w_up, w_down)


def _in_proj_sample_kernel(x_ref, g_ref, w_ref, b_ref, wkv_ref, bkv_ref,
                           cos_ref, sup_ref, sdn_ref, cosc_ref, supc_ref, sdnc_ref,
                           q_ref, kt_ref, vt_ref, u_ref):
    conv_ch = u_ref.shape[1]
    o2, o3 = Q_DIM + 2 * KV_DIM, Q_DIM + 2 * KV_DIM + conv_ch
    hn = _rms_norm_f32(x_ref[...], g_ref[...]).astype(BF16)
    zq = _dot(hn, w_ref[:, 0:Q_DIM]) + b_ref[:, 0:Q_DIM]
    q_ref[...] = _rope_lanes(zq, cos_ref[...], sup_ref[...], sdn_ref[...]) * ATTN_SCALE
    zkv = _dot_nt(wkv_ref[...], hn) + bkv_ref[...]
    kt_ref[...] = _rope(zkv[0:KV_DIM, :], cosc_ref[...], supc_ref[...], sdnc_ref[...], 0)
    vt_ref[...] = zkv[KV_DIM:, :]
    zu = _dot(hn, w_ref[:, o2:o3]) + b_ref[:, o2:o3]
    zg = _dot(hn, w_ref[:, o3:]) + b_ref[:, o3:]
    u_ref[...] = zu * jax.nn.sigmoid(zg)


def _in_proj_sample(x, g, w, b, wkv_t, bkv_c, row_tabs, col_tabs):
    n, d = x.shape
    conv_ch = (w.shape[1] - Q_DIM - 2 * KV_DIM) // 2
    full = lambda a: _resident(a.shape)
    args = (x, g, w, b, wkv_t, bkv_c, *row_tabs, *col_tabs)
    return pl.pallas_call(
        _in_proj_sample_kernel,
        grid=(1,),
        in_specs=[full(a) for a in args],
        out_specs=[_resident((n, Q_DIM)), _resident((KV_DIM, n)), _resident((KV_DIM, n)), _resident((n, conv_ch))],
        out_shape=[jax.ShapeDtypeStruct((n, Q_DIM), F32), jax.ShapeDtypeStruct((KV_DIM, n), F32),
                   jax.ShapeDtypeStruct((KV_DIM, n), F32), jax.ShapeDtypeStruct((n, conv_ch), F32)],
        compiler_params=_params("arbitrary"),
        name="in_proj_sample",
    )(*args)


def _attn_sample_kernel(q_ref, kn_ref, vn_ref, ck_ref, cv_ref, sink_ref, ao_ref, nk_ref, nv_ref):
    nseq, _, wb = ck_ref.shape
    rows = N_HEADS
    lane_group = lax.broadcasted_iota(jnp.int32, (rows, KV_DIM), 1) // HEAD_DIM
    row_group = lax.broadcasted_iota(jnp.int32, (rows, KV_DIM), 0) % N_KV_HEADS
    own_lanes = lane_group == row_group
    fold = (lax.broadcasted_iota(jnp.int32, (SUBLANES, rows), 1) // N_KV_HEADS
            == lax.broadcasted_iota(jnp.int32, (SUBLANES, rows), 0)).astype(F32)
    newest = lax.broadcasted_iota(jnp.int32, (KV_DIM, wb), 1) == wb - 1
    sink = sink_ref[...]
    kn, vn = kn_ref[0], vn_ref[0]

    seqs = range(nseq)
    knew = [jnp.where(newest, kn[:, n:n + 1], pltpu.roll(ck_ref[n], wb - 1, 1)) for n in seqs]
    vnew = [jnp.where(newest, vn[:, n:n + 1], pltpu.roll(cv_ref[n], wb - 1, 1)) for n in seqs]
    for n in seqs:
        nk_ref[n] = knew[n]
        nv_ref[n] = vnew[n]
    qrows = [jnp.where(own_lanes, q_ref[n], jnp.zeros((rows, KV_DIM), F32)).astype(BF16) for n in seqs]
    s = [_dot(qrows[n], knew[n].astype(BF16)) for n in seqs]
    p = [_sink_softmax(s[n], sink).astype(BF16) for n in seqs]
    o = [_dot_nt(p[n], vnew[n].astype(BF16)) for n in seqs]
    for n in seqs:
        ao_ref[n] = _dot(fold, jnp.where(own_lanes, o[n], jnp.zeros_like(o[n])))


def _attn_sample(q_rep, kn, vn, cache_kt, cache_vt, sink_rows, nseq):
    n, _, wb = cache_kt.shape
    r3 = lambda s: (s, 0, 0)
    return pl.pallas_call(
        _attn_sample_kernel,
        grid=(n // nseq,),
        in_specs=[pl.BlockSpec((nseq, N_HEADS, KV_DIM), r3),
                  pl.BlockSpec((1, KV_DIM, nseq), r3), pl.BlockSpec((1, KV_DIM, nseq), r3),
                  pl.BlockSpec((nseq, KV_DIM, wb), r3), pl.BlockSpec((nseq, KV_DIM, wb), r3),
                  _resident((N_HEADS, 1))],
        out_specs=[pl.BlockSpec((nseq, SUBLANES, KV_DIM), r3),
                   pl.BlockSpec((nseq, KV_DIM, wb), r3), pl.BlockSpec((nseq, KV_DIM, wb), r3)],
        out_shape=[jax.ShapeDtypeStruct((n, SUBLANES, KV_DIM), F32),
                   jax.ShapeDtypeStruct((n, KV_DIM, wb), F32), jax.ShapeDtypeStruct((n, KV_DIM, wb), F32)],
        compiler_params=_params("arbitrary"),
        name="attn_sample",
    )(q_rep, kn, vn, cache_kt, cache_vt, sink_rows)


def _conv_sample_kernel(cc_ref, cn_ref, u_ref, cw_ref, cb_ref, lg_ref, lb_ref, co_ref, nc_ref, acc_ref, *, hist):
    j = pl.program_id(0)
    u = u_ref[...]

    @pl.when(j == 0)
    def _():
        acc_ref[...] = jnp.zeros_like(acc_ref)

    acc_ref[...] += cw_ref[j] * cc_ref[0]

    @pl.when(j < hist - 1)
    def _():
        nc_ref[0] = cn_ref[0]

    @pl.when(j == hist - 1)
    def _():
        nc_ref[0] = u
        y = acc_ref[...] + cw_ref[hist] * u + cb_ref[...]
        co_ref[...] = _layer_norm_swish(y, lg_ref[...], lb_ref[...]).astype(co_ref.dtype)


def _conv_sample(cache_t, u, conv_w, conv_b, ln_g, ln_b):
    hist, n, ch = cache_t.shape
    tap = lambda j: (j, 0, 0)
    return pl.pallas_call(
        functools.partial(_conv_sample_kernel, hist=hist),
        grid=(hist,),
        in_specs=[pl.BlockSpec((1, n, ch), tap),
                  pl.BlockSpec((1, n, ch), lambda j: (jnp.minimum(j + 1, hist - 1), 0, 0)),
                  _resident((n, ch)), _resident((hist + 1, 1, ch)),
                  _resident((1, ch)), _resident((1, ch)), _resident((1, ch))],
        out_specs=[_resident((n, ch)), pl.BlockSpec((1, n, ch), tap)],
        out_shape=[jax.ShapeDtypeStruct((n, ch), BF16), jax.ShapeDtypeStruct((hist, n, ch), F32)],
        scratch_shapes=[pltpu.VMEM((n, ch), F32)],
        compiler_params=_params("arbitrary"),
        name="conv_sample",
    )(cache_t, cache_t, u, conv_w.reshape(hist + 1, 1, ch), conv_b, ln_g, ln_b)


def _out_mlp_kernel(x_ref, a_ref, c_ref, wo_ref, bo_ref, g2_ref, wu_ref, wd_ref, gf_ref,
                    o_ref, hn_ref, *, final_norm):
    f = pl.program_id(1)

    @pl.when(f == 0)
    def _():
        mix = [ma + mc for ma, mc in zip(_dot_chunks(a_ref[...], wo_ref, slice(0, Q_DIM)),
                                         _dot_chunks(c_ref[...], wo_ref, slice(Q_DIM, None)))]
        h = x_ref[...] + jnp.concatenate(mix, axis=1) + bo_ref[...]
        o_ref[...] = h
        hn_ref[...] = _rms_norm_f32(h, g2_ref[...]).astype(hn_ref.dtype)

    up = jnp.concatenate(_dot_chunks(hn_ref[...], wu_ref), axis=1)
    act = jnp.square(jnp.maximum(up, 0.0)).astype(BF16)
    for c, down in enumerate(_dot_chunks(act, wd_ref)):
        o_ref[:, c * COL_CHUNK:(c + 1) * COL_CHUNK] += down

    if final_norm:
        @pl.when(f == pl.num_programs(1) - 1)
        def _():
            o_ref[...] = _rms_norm_f32(o_ref[...], gf_ref[...])


def _out_mlp(x, a, c, w_out, b_out, g2, w_up, w_down, gf, tm, tf, final_norm):
    m, d = x.shape
    conv_ch = w_out.shape[1] - Q_DIM
    ff = w_down.shape[1]
    row = lambda i, f: (i, 0)
    return pl.pallas_call(
        functools.partial(_out_mlp_kernel, final_norm=final_norm),
        grid=(m // tm, ff // tf),
        in_specs=[pl.BlockSpec((tm, d), row), pl.BlockSpec((tm, Q_DIM), row),
                  pl.BlockSpec((tm, conv_ch), row),
                  _resident(w_out.shape), _resident((1, d)), _resident((1, d)),
                  pl.BlockSpec((tf // COL_CHUNK, d, COL_CHUNK), lambda i, f: (f, 0, 0)),
                  pl.BlockSpec((w_down.shape[0], tf, COL_CHUNK), lambda i, f: (0, f, 0)),
                  _resident((1, d))],
        out_specs=pl.BlockSpec((tm, d), row),
        out_shape=jax.ShapeDtypeStruct((m, d), F32),
        scratch_shapes=[pltpu.VMEM((tm, d), BF16)],
        compiler_params=_params("arbitrary", "arbitrary"),
        name="out_mlp",
    )(x, a, c, w_out, b_out, g2, w_up, w_down, gf)


def _rope_tables(pos):
    half = ROT_DIM // 2
    inv_freq = jnp.power(jnp.float32(ROPE_THETA), -jnp.arange(half, dtype=jnp.float32) * 2.0 / ROT_DIM)
    ang = pos.astype(jnp.float32)[:, None] * inv_freq[None, :]
    cos, sin = jnp.cos(ang), jnp.sin(ang)
    t = pos.shape[0]
    pad = jnp.zeros((t, HEAD_DIM - ROT_DIM), F32)
    zeros = jnp.zeros((t, half), F32)
    per_head = lambda a, b, fill: jnp.concatenate([a, b, pad + fill], axis=1)
    reps = LANES // HEAD_DIM
    cos_t = jnp.tile(per_head(cos, cos, 1.0), (1, reps))
    sup_t = jnp.tile(per_head(zeros, sin, 0.0), (1, reps))
    sdn_t = jnp.tile(per_head(-sin, zeros, 0.0), (1, reps))
    return cos_t, sup_t, sdn_t


def _heads_to_igd(a, axis):
    shape = a.shape
    a = a.reshape(shape[:axis] + (N_KV_HEADS, GROUP, HEAD_DIM) + shape[axis + 1:])
    return jnp.swapaxes(a, axis, axis + 1).reshape(shape)


def _cast_slabs_kernel(w_ref, o_ref):
    o_ref[0] = w_ref[...].astype(o_ref.dtype)


def _cast_slabs(w, tk):
    k, n = w.shape
    return pl.pallas_call(
        _cast_slabs_kernel,
        grid=(n // COL_CHUNK, k // tk),
        in_specs=[pl.BlockSpec((tk, COL_CHUNK), lambda j, kk: (kk, j))],
        out_specs=pl.BlockSpec((1, tk, COL_CHUNK), lambda j, kk: (j, kk, 0)),
        out_shape=jax.ShapeDtypeStruct((n // COL_CHUNK, k, COL_CHUNK), BF16),
        compiler_params=_params("arbitrary", "arbitrary"),
        name="cast_slabs",
    )(w)


def _igd_source(idx):
    i, g, d = idx // KV_DIM, (idx // HEAD_DIM) % N_KV_HEADS, idx % HEAD_DIM
    return (g * GROUP + i) * HEAD_DIM + d


def _prep_w_in_kernel(w_ref, o_ref):
    w = w_ref[...].astype(o_ref.dtype)
    src = lax.broadcasted_iota(jnp.int32, (Q_DIM, Q_DIM), 0)
    dst = lax.broadcasted_iota(jnp.int32, (Q_DIM, Q_DIM), 1)
    move = jnp.where(src == _igd_source(dst), 1.0, 0.0).astype(o_ref.dtype)
    o_ref[:, 0:Q_DIM] = _dot(w[:, 0:Q_DIM], move).astype(o_ref.dtype)
    o_ref[:, Q_DIM:] = w[:, Q_DIM:]


def _prep_w_in(w, tk):
    k, n = w.shape
    return pl.pallas_call(
        _prep_w_in_kernel,
        grid=(k // tk,),
        in_specs=[pl.BlockSpec((tk, n), lambda j: (j, 0))],
        out_specs=pl.BlockSpec((tk, n), lambda j: (j, 0)),
        out_shape=jax.ShapeDtypeStruct((k, n), BF16),
        compiler_params=_params("arbitrary"),
        name="prep_w_in",
    )(w)


def _prep_w_out_kernel(w_ref, o_ref):
    w = w_ref[...].astype(o_ref.dtype)
    dst = lax.broadcasted_iota(jnp.int32, (Q_DIM, Q_DIM), 0)
    src = lax.broadcasted_iota(jnp.int32, (Q_DIM, Q_DIM), 1)
    move = jnp.where(src == _igd_source(dst), 1.0, 0.0).astype(o_ref.dtype)
    o_ref[0, 0:Q_DIM, :] = _dot(move, w[0:Q_DIM, :]).astype(o_ref.dtype)
    o_ref[0, Q_DIM:, :] = w[Q_DIM:, :]


def _prep_w_out(w):
    k, n = w.shape
    return pl.pallas_call(
        _prep_w_out_kernel,
        grid=(n // COL_CHUNK,),
        in_specs=[pl.BlockSpec((k, COL_CHUNK), lambda c: (0, c))],
        out_specs=pl.BlockSpec((1, k, COL_CHUNK), lambda c: (c, 0, 0)),
        out_shape=jax.ShapeDtypeStruct((n // COL_CHUNK, k, COL_CHUNK), BF16),
        compiler_params=_params("arbitrary"),
        name="prep_w_out",
    )(w)


def _tile(m, target):
    t = min(m, target)
    assert m % t == 0, (m, t)
    return t


def kernel(x_prompt, x_sample, cache_k, cache_v, cache_conv, norm1_g, w_in, b_in, attn_sinks,
           conv_w, conv_b, conv_ln_g, conv_ln_b, w_out, b_out, norm2_g, w_up, w_down, final_norm_g):
    batch, seq, d_model = x_prompt.shape
    nsamp, t_s, _ = x_sample.shape
    depth = w_in.shape[0]
    conv_ch = conv_w.shape[2]
    wb = cache_k.shape[2]
    hist = CONV_W - 1
    assert t_s == 1 and seq % WINDOW == 0 and wb == WINDOW and cache_conv.shape[2] == hist
    assert depth >= 1

    tabs_p = _rope_tables(jnp.arange(seq, dtype=jnp.int32))
    tabs_s1 = _rope_tables(PAST_LEN + jnp.arange(t_s, dtype=jnp.int32))
    tabs_s = tuple(jnp.broadcast_to(t, (nsamp, LANES)) for t in tabs_s1)
    tabs_sc = tuple(jnp.tile(t[0], KV_DIM // LANES).reshape(KV_DIM, 1) for t in tabs_s1)
    row = lambda a: a.reshape(1, -1)

    hp = x_prompt.reshape(batch * seq, d_model)
    hs = x_sample.reshape(nsamp * t_s, d_model)
    tm_in = _tile(seq, 512)
    tm_mlp = _tile(batch * seq, 512)
    tf_s = 2 * COL_CHUNK
    nseq = _tile(nsamp, 8)
    o1, o2 = Q_DIM, Q_DIM + 2 * KV_DIM
    pk, pv, pc, sk, sv, sc = [], [], [], [], [], []
    for l in range(depth):
        last = l == depth - 1
        w_in_l = _prep_w_in(w_in[l], _tile(d_model, 512))
        b_in_l = row(jnp.concatenate([_heads_to_igd(b_in[l][:o1], 0), b_in[l][o1:]]))
        w_out_l = _prep_w_out(w_out[l])
        w_up_l, w_down_l = _cast_slabs(w_up[l], _tile(d_model, 2048)), _cast_slabs(w_down[l], 2048)
        sinks = attn_sinks[l].astype(F32)
        sink_rows = sinks.reshape(N_KV_HEADS, GROUP).T.reshape(N_HEADS, 1)
        g1, g2, gf = row(norm1_g[l]), row(norm2_g[l]), row(final_norm_g)
        cb, lg, lb, bo = row(conv_b[l]), row(conv_ln_g[l]), row(conv_ln_b[l]), row(b_out[l])

        q, k, v, u, pk_l, pv_l, pc_l = _in_proj(hp, g1, w_in_l, b_in_l, *tabs_p, tm_in, batch)
        by_part = lambda a: a.reshape(a.shape[0], PARTS, KV_DIM).transpose(1, 0, 2)
        hp = _mlp_mix(hp, q, k, v, u, sinks, by_part(conv_w[l]), by_part(cb), lg, lb, w_out_l, bo, g2, gf,
                      w_up_l, w_down_l, seq, tm_mlp, 1, last)
        pk.append(pk_l.reshape(batch, WINDOW, N_KV_HEADS, HEAD_DIM))
        pv.append(pv_l.reshape(batch, WINDOW, N_KV_HEADS, HEAD_DIM))
        pc.append(pc_l)

        cache_kt = jnp.transpose(cache_k[l], (0, 2, 3, 1)).reshape(nsamp, KV_DIM, wb)
        cache_vt = jnp.transpose(cache_v[l], (0, 2, 3, 1)).reshape(nsamp, KV_DIM, wb)
        cache_ct = jnp.transpose(cache_conv[l], (1, 0, 2))
        wkv_t = w_in[l][:, o1:o2].T.astype(BF16)
        bkv_c = b_in[l][o1:o2].reshape(2 * KV_DIM, 1)
        q, kt, vt, u = _in_proj_sample(hs, g1, w_in_l, b_in_l, wkv_t, bkv_c, tabs_s, tabs_sc)
        q_rep = jnp.repeat(q.reshape(nsamp, GROUP, KV_DIM), N_KV_HEADS, axis=1)
        by_block = lambda a: a.reshape(KV_DIM, nsamp // nseq, nseq).transpose(1, 0, 2)
        a8, nkt, nvt = _attn_sample(q_rep, by_block(kt), by_block(vt), cache_kt, cache_vt, sink_rows, nseq)
        c_o, nct = _conv_sample(cache_ct, u, conv_w[l], cb, lg, lb)
        a_o = a8[:, :GROUP, :].reshape(nsamp, Q_DIM).astype(BF16)
        hs = _out_mlp(hs, a_o, c_o, w_out_l, bo, g2, w_up_l, w_down_l, gf, nsamp, tf_s, last)
        to_cache = lambda a: jnp.transpose(a.reshape(nsamp, N_KV_HEADS, HEAD_DIM, wb), (0, 3, 1, 2))
        sk.append(to_cache(nkt))
        sv.append(to_cache(nvt))
        sc.append(jnp.transpose(nct, (1, 0, 2)))

    y_prompt = hp.reshape(batch, seq, d_model)
    y_sample = hs.reshape(nsamp, t_s, d_model)
    return (y_prompt, y_sample, jnp.stack(pk), jnp.stack(pv), jnp.stack(pc),
            jnp.stack(sk), jnp.stack(sv), jnp.stack(sc))
```

```python
import functools

import jax
import jax.numpy as jnp
import numpy as np
from jax import lax
from jax.experimental import pallas as pl
from jax.experimental.pallas import tpu as pltpu

HEAD_DIM = 64
N_HEADS = 16
N_KV_HEADS = 4
GROUP = N_HEADS // N_KV_HEADS
WINDOW = 128
ROT_DIM = HEAD_DIM // 4
ROPE_THETA = 500000.0
ATTN_SCALE = HEAD_DIM ** -0.5
Q_DIM = N_HEADS * HEAD_DIM
KV_DIM = N_KV_HEADS * HEAD_DIM
CONV_W = 31
PAST_LEN = 16384
EPS = 1e-5

LANES = 128
SUBLANES = 8
CONV_HALO = 32
CONV_ROWS = 64
VMEM_LIMIT_BYTES = 60 * 1024 * 1024
PARTS = GROUP
COL_CHUNK = 512

BF16 = jnp.bfloat16
F32 = jnp.float32


def _resident(shape):
    return pl.BlockSpec(shape, lambda *_: (0,) * len(shape), pipeline_mode=pl.Buffered(1))


def _params(*semantics):
    return pltpu.CompilerParams(dimension_semantics=semantics, vmem_limit_bytes=VMEM_LIMIT_BYTES)


def _rms_norm_f32(x, g):
    return x * lax.rsqrt(jnp.mean(x * x, axis=-1, keepdims=True) + EPS) * g


def _layer_norm_swish(y, g, b):
    mu = jnp.mean(y, axis=-1, keepdims=True)
    yc = y - mu
    var = jnp.mean(yc * yc, axis=-1, keepdims=True)
    yn = yc * lax.rsqrt(var + EPS) * g + b
    return yn * jax.nn.sigmoid(yn)


def _dot(a, b):
    return jnp.dot(a, b, preferred_element_type=F32)


def _dot_nt(a, b):
    return lax.dot_general(a, b, (((1,), (1,)), ((), ())), preferred_element_type=F32)


def _dot_chunks(a, w_ref, rows=slice(None)):
    return [_dot(a, w_ref[c, rows, :]) for c in range(w_ref.shape[0])]


def _sink_softmax(s, sink):
    m = jnp.maximum(jnp.max(s, axis=-1, keepdims=True), sink)
    p = jnp.exp(s - m)
    return p / (jnp.sum(p, axis=-1, keepdims=True) + jnp.exp(sink - m))


def _rope(z, cos, sin_up, sin_dn, axis):
    n = z.shape[axis]
    half = ROT_DIM // 2
    return z * cos + pltpu.roll(z, half, axis) * sin_up + pltpu.roll(z, n - half, axis) * sin_dn


def _rope_lanes(z, cos, sin_up, sin_dn):
    cols = [_rope(z[:, c * LANES:(c + 1) * LANES], cos, sin_up, sin_dn, 1) for c in range(z.shape[1] // LANES)]
    return jnp.concatenate(cols, axis=1)


def _in_proj_kernel(x_ref, g_ref, w_ref, b_ref, cos_ref, sup_ref, sdn_ref,
                    q_ref, k_ref, v_ref, u_ref, pk_ref, pv_ref, pc_ref):
    conv_ch = u_ref.shape[0] * u_ref.shape[2]
    tm = x_ref.shape[0]
    hist = pc_ref.shape[1]
    o1, o2, o3, o4 = Q_DIM, Q_DIM + KV_DIM, Q_DIM + 2 * KV_DIM, Q_DIM + 2 * KV_DIM + conv_ch
    hn = _rms_norm_f32(x_ref[...], g_ref[...]).astype(BF16)
    cos, sup, sdn = cos_ref[...], sup_ref[...], sdn_ref[...]

    zq = _dot(hn, w_ref[:, 0:o1]) + b_ref[:, 0:o1]
    q = (_rope_lanes(zq, cos, sup, sdn) * ATTN_SCALE).astype(q_ref.dtype)
    zk = _dot(hn, w_ref[:, o1:o2]) + b_ref[:, o1:o2]
    k_ref[...] = _rope_lanes(zk, cos, sup, sdn)
    v_ref[...] = _dot(hn, w_ref[:, o2:o3]) + b_ref[:, o2:o3]
    zu = _dot(hn, w_ref[:, o3:o4]) + b_ref[:, o3:o4]
    zg = _dot(hn, w_ref[:, o4:]) + b_ref[:, o4:]
    u = zu * jax.nn.sigmoid(zg)
    for p in range(PARTS):
        q_ref[p] = q[:, p * KV_DIM:(p + 1) * KV_DIM]
        u_ref[p] = u[:, p * KV_DIM:(p + 1) * KV_DIM]
    pk_ref[0] = k_ref[tm - WINDOW:, :]
    pv_ref[0] = v_ref[tm - WINDOW:, :]
    pc_ref[0] = u[tm - hist:, :]


def _in_proj(x, g, w, b, cos, sup, sdn, tm, batch):
    m, d = x.shape
    n = w.shape[1]
    conv_ch = (n - Q_DIM - 2 * KV_DIM) // 2
    nt = cos.shape[0] // tm
    hist = CONV_W - 1
    row = lambda i: (i, 0)
    parts = lambda i: (0, i, 0)
    tab = lambda i: (i % nt, 0)
    seq = lambda i: (i // nt, 0, 0)
    assert conv_ch == PARTS * KV_DIM and Q_DIM == PARTS * KV_DIM
    return pl.pallas_call(
        _in_proj_kernel,
        grid=(m // tm,),
        in_specs=[pl.BlockSpec((tm, d), row), _resident((1, d)), _resident((d, n)), _resident((1, n)),
                  pl.BlockSpec((tm, LANES), tab), pl.BlockSpec((tm, LANES), tab),
                  pl.BlockSpec((tm, LANES), tab)],
        out_specs=[pl.BlockSpec((PARTS, tm, KV_DIM), parts), pl.BlockSpec((tm, KV_DIM), row),
                   pl.BlockSpec((tm, KV_DIM), row), pl.BlockSpec((PARTS, tm, KV_DIM), parts),
                   pl.BlockSpec((1, WINDOW, KV_DIM), seq), pl.BlockSpec((1, WINDOW, KV_DIM), seq),
                   pl.BlockSpec((1, hist, conv_ch), seq)],
        out_shape=[jax.ShapeDtypeStruct((PARTS, m, KV_DIM), BF16), jax.ShapeDtypeStruct((m, KV_DIM), F32),
                   jax.ShapeDtypeStruct((m, KV_DIM), F32), jax.ShapeDtypeStruct((PARTS, m, KV_DIM), F32),
                   jax.ShapeDtypeStruct((batch, WINDOW, KV_DIM), F32),
                   jax.ShapeDtypeStruct((batch, WINDOW, KV_DIM), F32),
                   jax.ShapeDtypeStruct((batch, hist, conv_ch), F32)],
        compiler_params=_params("arbitrary"),
        name="in_proj",
    )(x, g, w, b, cos, sup, sdn)


def _attn_scores(q, kk, vv, has_prev):
    blk = q.shape[0]
    kk, vv = kk.astype(BF16), vv.astype(BF16)
    lane_group = lax.broadcasted_iota(jnp.int32, kk.shape, 1) // HEAD_DIM
    zero = jnp.zeros_like(kk)
    kstack = jnp.concatenate([jnp.where(lane_group == g, kk, zero) for g in range(N_KV_HEADS)], axis=0)
    vstack = jnp.concatenate([jnp.where(lane_group == g, vv, zero) for g in range(N_KV_HEADS)], axis=0)
    qi = lax.broadcasted_iota(jnp.int32, (blk, 2 * blk), 0)
    kj = lax.broadcasted_iota(jnp.int32, (blk, 2 * blk), 1)
    rel = qi + blk - kj
    valid = (rel >= 0) & (rel < WINDOW) & (has_prev | (kj >= blk))
    return _dot_nt(q, kstack), valid, vstack


def _attn_probs(s_all, valid, sink_ref, part, zero_tile):
    width = valid.shape[1]
    probs = []
    for g in range(N_KV_HEADS):
        s = jnp.where(valid, s_all[:, g * width:(g + 1) * width], -jnp.inf)
        p = _sink_softmax(s, sink_ref[g * GROUP + part])
        if g == 0:
            top = jnp.concatenate([p[0:SUBLANES, 0:LANES] + zero_tile, p[0:SUBLANES, LANES:]], axis=1)
            p = jnp.concatenate([top, p[SUBLANES:, :]], axis=0)
        probs.append(p.astype(BF16))
    return jnp.concatenate(probs, axis=1)


def _conv_lanes(c, cw_ref, cb_ref, ue_ref, row0, blk, c_out):
    first = CONV_HALO - (CONV_W - 1)
    cs = slice(c * LANES, (c + 1) * LANES)
    seen = None
    for s0 in range(0, blk, CONV_ROWS):
        win = ue_ref[pl.ds(pl.multiple_of(row0 + s0, SUBLANES), CONV_ROWS + CONV_HALO), cs]
        out = cb_ref[:, cs]
        for r in range(SUBLANES):
            rows = CONV_ROWS + (SUBLANES if r else 0)
            part = None
            for a in range((first + CONV_W - 1 - r) // SUBLANES + 1):
                t = a * SUBLANES + r - first
                if 0 <= t < CONV_W:
                    term = cw_ref[t:t + 1, cs] * win[a * SUBLANES:a * SUBLANES + rows, :]
                    part = term if part is None else part + term
            out = out + part[r:r + CONV_ROWS, :]
        c_out[s0:s0 + CONV_ROWS, cs] = out
        for r0 in range(0, CONV_ROWS, SUBLANES):
            tile = out[r0:r0 + SUBLANES, :]
            seen = tile if seen is None else jnp.maximum(seen, tile)
    return seen


def _mlp_mix_kernel(x_ref, q_ref, k_ref, kp_ref, v_ref, vp_ref, u_ref, uh_ref, sink_ref, cw_ref, cb_ref,
                    lg_ref, lb_ref, wo_ref, bo_ref, g2_ref, gf_ref, wu_ref, wd_ref,
                    o_ref, hn_ref, a_s, c_s, kbuf, vbuf, ubuf, *, nb, nrow, parts_per_step, final_norm):
    i, f = pl.program_id(0), pl.program_id(1)
    blk = WINDOW
    blocks_per_tile = x_ref.shape[0] // blk
    steps_per_block = PARTS // parts_per_step

    @pl.when((i == 0) & (f == 0))
    def _():
        a_s[...] = jnp.zeros_like(a_s)
        c_s[...] = jnp.zeros_like(c_s)

    @pl.when(f == 0)
    def _():
        a = jnp.concatenate([a_s[p] for p in range(PARTS)], axis=1)
        conv = jnp.concatenate([c_s[p] for p in range(PARTS)], axis=1)
        c = _layer_norm_swish(conv, lg_ref[...], lb_ref[...]).astype(BF16)
        mix = [ma + mc for ma, mc in zip(_dot_chunks(a, wo_ref, slice(0, Q_DIM)),
                                         _dot_chunks(c, wo_ref, slice(Q_DIM, None)))]
        h = x_ref[...] + jnp.concatenate(mix, axis=1) + bo_ref[...]
        o_ref[...] = h
        hn_ref[...] = _rms_norm_f32(h, g2_ref[...]).astype(hn_ref.dtype)
        kbuf[0:blk, :] = kp_ref[...]
        kbuf[blk:, :] = k_ref[...]
        vbuf[0:blk, :] = vp_ref[...]
        vbuf[blk:, :] = v_ref[...]
        continues = (jnp.minimum(i, nrow - 1) * blocks_per_tile) % nb != 0
        for p in range(PARTS):
            ubuf[p, 0:CONV_HALO, :] = jnp.where(continues, uh_ref[p], jnp.zeros_like(uh_ref[p]))
            ubuf[p, CONV_HALO:, :] = u_ref[p]

    sub, part0 = f // steps_per_block, (f % steps_per_block) * parts_per_step
    parts = [part0 + j for j in range(parts_per_step)]
    gblk = jnp.minimum(i, nrow - 1) * blocks_per_tile + sub
    has_prev = gblk % nb != 0
    rows = pl.ds(pl.multiple_of(sub * blk, blk), blk)
    both = pl.ds(pl.multiple_of(sub * blk, blk), 2 * blk)
    conv_done = []
    for part in parts:
        conv_out = c_s.at[part, rows, :]
        seen = jnp.maximum(_conv_lanes(0, cw_ref.at[part], cb_ref.at[part], ubuf.at[part], sub * blk, blk, conv_out),
                           _conv_lanes(1, cw_ref.at[part], cb_ref.at[part], ubuf.at[part], sub * blk, blk, conv_out))
        conv_done.append(jnp.minimum(jnp.abs(seen), 0.0))
    scores = [_attn_scores(q_ref[part, rows, :], kbuf[both, :], vbuf[both, :], has_prev) for part in parts]
    up = jnp.concatenate(_dot_chunks(hn_ref[...], wu_ref), axis=1)
    act = jnp.square(jnp.maximum(up, 0.0)).astype(BF16)
    for part, (s_all, valid, vstack), zero in zip(parts, scores, conv_done):
        probs = _attn_probs(s_all, valid, sink_ref, part, zero)
        a_s[part, rows, :] = _dot(probs, vstack).astype(a_s.dtype)
    for c, down in enumerate(_dot_chunks(act, wd_ref)):
        o_ref[:, c * COL_CHUNK:(c + 1) * COL_CHUNK] += down

    if final_norm:
        @pl.when(f == pl.num_programs(1) - 1)
        def _():
            o_ref[...] = _rms_norm_f32(o_ref[...], gf_ref[...])


def _mlp_mix(x, q, k, v, u, sinks, conv_w, conv_b, ln_g, ln_b, w_out, b_out, g2, gf, w_up, w_down,
             seq, tm, parts_per_step, final_norm):
    m, d = x.shape
    conv_ch = u.shape[0] * u.shape[2]
    ff = w_down.shape[1]
    blk = WINDOW
    nb = seq // blk
    nrow = m // tm
    bpt = tm // blk
    nf = bpt * PARTS // parts_per_step
    assert ff % (nf * COL_CHUNK) == 0 and conv_ch == PARTS * KV_DIM and Q_DIM == PARTS * KV_DIM
    assert seq % tm == 0
    tf = ff // nf
    hpb = blk // CONV_HALO
    mix_row = lambda i: jnp.minimum(i, nrow - 1)
    mlp_tile = lambda i, f: (jnp.maximum(i - 1, 0), 0)
    mix_tile = lambda i, f: (mix_row(i), 0)
    mix_parts = lambda i, f: (0, mix_row(i), 0)
    prev = lambda i, f: (jnp.maximum(mix_row(i) * bpt - 1, 0), 0)
    halo_parts = lambda i, f: (0, jnp.maximum(mix_row(i) * bpt * hpb - 1, 0), 0)
    return pl.pallas_call(
        functools.partial(_mlp_mix_kernel, nb=nb, nrow=nrow, parts_per_step=parts_per_step,
                          final_norm=final_norm),
        grid=(nrow + 1, nf),
        in_specs=[pl.BlockSpec((tm, d), mlp_tile),
                  pl.BlockSpec((PARTS, tm, KV_DIM), mix_parts),
                  pl.BlockSpec((tm, KV_DIM), mix_tile), pl.BlockSpec((blk, KV_DIM), prev),
                  pl.BlockSpec((tm, KV_DIM), mix_tile), pl.BlockSpec((blk, KV_DIM), prev),
                  pl.BlockSpec((PARTS, tm, KV_DIM), mix_parts),
                  pl.BlockSpec((PARTS, CONV_HALO, KV_DIM), halo_parts),
                  pl.BlockSpec(memory_space=pltpu.SMEM),
                  _resident((PARTS, CONV_W, KV_DIM)), _resident((PARTS, 1, KV_DIM)),
                  _resident((1, conv_ch)), _resident((1, conv_ch)),
                  _resident(w_out.shape), _resident((1, d)), _resident((1, d)), _resident((1, d)),
                  pl.BlockSpec((tf // COL_CHUNK, d, COL_CHUNK), lambda i, f: (f, 0, 0)),
                  pl.BlockSpec((w_down.shape[0], tf, COL_CHUNK), lambda i, f: (0, f, 0))],
        out_specs=pl.BlockSpec((tm, d), mlp_tile),
        out_shape=jax.ShapeDtypeStruct((m, d), F32),
        scratch_shapes=[pltpu.VMEM((tm, d), BF16),
                        pltpu.VMEM((PARTS, tm, KV_DIM), BF16),
                        pltpu.VMEM((PARTS, tm, KV_DIM), F32),
                        pltpu.VMEM((blk + tm, KV_DIM), F32), pltpu.VMEM((blk + tm, KV_DIM), F32),
                        pltpu.VMEM((PARTS, CONV_HALO + tm, KV_DIM), F32)],
        compiler_params=_params("arbitrary", "arbitrary"),
        name="mlp_mix",
    )(x, q, k, k, v, v, u, u, sinks, conv_w, conv_b, ln_g, ln_b, w_out, b_out, g2, gf, w_up, w_down)


def _in_proj_sample_kernel(x_ref, g_ref, w_ref, b_ref, bkv_ref,
                           cos_ref, sup_ref, sdn_ref, cosc_ref, supc_ref, sdnc_ref,
                           q_ref, kt_ref, vt_ref, u_ref):
    conv_ch = u_ref.shape[1]
    o2, o3 = Q_DIM + 2 * KV_DIM, Q_DIM + 2 * KV_DIM + conv_ch
    hn = _rms_norm_f32(x_ref[...], g_ref[...]).astype(BF16)
    zq = _dot(hn, w_ref[:, 0:Q_DIM]) + b_ref[:, 0:Q_DIM]
    q_ref[...] = _rope_lanes(zq, cos_ref[...], sup_ref[...], sdn_ref[...]) * ATTN_SCALE
    zkv = lax.dot_general(w_ref[:, Q_DIM:o2], hn, (((0,), (1,)), ((), ())), preferred_element_type=F32) + bkv_ref[...]
    kt_ref[...] = _rope(zkv[0:KV_DIM, :], cosc_ref[...], supc_ref[...], sdnc_ref[...], 0)
    vt_ref[...] = zkv[KV_DIM:, :]
    zu = _dot(hn, w_ref[:, o2:o3]) + b_ref[:, o2:o3]
    zg = _dot(hn, w_ref[:, o3:]) + b_ref[:, o3:]
    u_ref[...] = zu * jax.nn.sigmoid(zg)


def _in_proj_sample(x, g, w, b, bkv_c, row_tabs, col_tabs):
    n, d = x.shape
    conv_ch = (w.shape[1] - Q_DIM - 2 * KV_DIM) // 2
    full = lambda a: _resident(a.shape)
    args = (x, g, w, b, bkv_c, *row_tabs, *col_tabs)
    return pl.pallas_call(
        _in_proj_sample_kernel,
        grid=(1,),
        in_specs=[full(a) for a in args],
        out_specs=[_resident((n, Q_DIM)), _resident((KV_DIM, n)), _resident((KV_DIM, n)), _resident((n, conv_ch))],
        out_shape=[jax.ShapeDtypeStruct((n, Q_DIM), F32), jax.ShapeDtypeStruct((KV_DIM, n), F32),
                   jax.ShapeDtypeStruct((KV_DIM, n), F32), jax.ShapeDtypeStruct((n, conv_ch), F32)],
        compiler_params=_params("arbitrary"),
        name="in_proj_sample",
    )(*args)


def _attn_sample_kernel(q_ref, kn_ref, vn_ref, ck_ref, cv_ref, sink_ref, ao_ref, nk_ref, nv_ref):
    nseq, _, wb = ck_ref.shape
    rows = N_HEADS
    lane_group = lax.broadcasted_iota(jnp.int32, (rows, KV_DIM), 1) // HEAD_DIM
    row_group = lax.broadcasted_iota(jnp.int32, (rows, KV_DIM), 0) % N_KV_HEADS
    own_lanes = lane_group == row_group
    fold = (lax.broadcasted_iota(jnp.int32, (SUBLANES, rows), 1) // N_KV_HEADS
            == lax.broadcasted_iota(jnp.int32, (SUBLANES, rows), 0)).astype(F32)
    newest = lax.broadcasted_iota(jnp.int32, (KV_DIM, wb), 1) == wb - 1
    sink = sink_ref[...]
    kn, vn = kn_ref[0], vn_ref[0]

    seqs = range(nseq)
    knew = [jnp.where(newest, kn[:, n:n + 1], pltpu.roll(ck_ref[n], wb - 1, 1)) for n in seqs]
    vnew = [jnp.where(newest, vn[:, n:n + 1], pltpu.roll(cv_ref[n], wb - 1, 1)) for n in seqs]
    for n in seqs:
        nk_ref[n] = knew[n]
        nv_ref[n] = vnew[n]
    qrows = [jnp.where(own_lanes, q_ref[n], jnp.zeros((rows, KV_DIM), F32)).astype(BF16) for n in seqs]
    s = [_dot(qrows[n], knew[n].astype(BF16)) for n in seqs]
    p = [_sink_softmax(s[n], sink).astype(BF16) for n in seqs]
    o = [_dot_nt(p[n], vnew[n].astype(BF16)) for n in seqs]
    for n in seqs:
        ao_ref[n] = _dot(fold, jnp.where(own_lanes, o[n], jnp.zeros_like(o[n])))


def _attn_sample(q_rep, kn, vn, cache_kt, cache_vt, sink_rows, nseq):
    n, _, wb = cache_kt.shape
    r3 = lambda s: (s, 0, 0)
    return pl.pallas_call(
        _attn_sample_kernel,
        grid=(n // nseq,),
        in_specs=[pl.BlockSpec((nseq, N_HEADS, KV_DIM), r3),
                  pl.BlockSpec((1, KV_DIM, nseq), r3), pl.BlockSpec((1, KV_DIM, nseq), r3),
                  pl.BlockSpec((nseq, KV_DIM, wb), r3), pl.BlockSpec((nseq, KV_DIM, wb), r3),
                  _resident((N_HEADS, 1))],
        out_specs=[pl.BlockSpec((nseq, SUBLANES, KV_DIM), r3),
                   pl.BlockSpec((nseq, KV_DIM, wb), r3), pl.BlockSpec((nseq, KV_DIM, wb), r3)],
        out_shape=[jax.ShapeDtypeStruct((n, SUBLANES, KV_DIM), F32),
                   jax.ShapeDtypeStruct((n, KV_DIM, wb), F32), jax.ShapeDtypeStruct((n, KV_DIM, wb), F32)],
        compiler_params=_params("arbitrary"),
        name="attn_sample",
    )(q_rep, kn, vn, cache_kt, cache_vt, sink_rows)


def _conv_sample_kernel(cc_ref, cn_ref, u_ref, cw_ref, cb_ref, lg_ref, lb_ref, co_ref, nc_ref, acc_ref, *, hist):
    j = pl.program_id(0)
    u = u_ref[...]

    @pl.when(j == 0)
    def _():
        acc_ref[...] = jnp.zeros_like(acc_ref)

    acc_ref[...] += cw_ref[j] * cc_ref[0]

    @pl.when(j < hist - 1)
    def _():
        nc_ref[0] = cn_ref[0]

    @pl.when(j == hist - 1)
    def _():
        nc_ref[0] = u
        y = acc_ref[...] + cw_ref[hist] * u + cb_ref[...]
        co_ref[...] = _layer_norm_swish(y, lg_ref[...], lb_ref[...]).astype(co_ref.dtype)


def _conv_sample(cache_t, u, conv_w, conv_b, ln_g, ln_b):
    hist, n, ch = cache_t.shape
    tap = lambda j: (j, 0, 0)
    return pl.pallas_call(
        functools.partial(_conv_sample_kernel, hist=hist),
        grid=(hist,),
        in_specs=[pl.BlockSpec((1, n, ch), tap),
                  pl.BlockSpec((1, n, ch), lambda j: (jnp.minimum(j + 1, hist - 1), 0, 0)),
                  _resident((n, ch)), _resident((hist + 1, 1, ch)),
                  _resident((1, ch)), _resident((1, ch)), _resident((1, ch))],
        out_specs=[_resident((n, ch)), pl.BlockSpec((1, n, ch), tap)],
        out_shape=[jax.ShapeDtypeStruct((n, ch), BF16), jax.ShapeDtypeStruct((hist, n, ch), F32)],
        scratch_shapes=[pltpu.VMEM((n, ch), F32)],
        compiler_params=_params("arbitrary"),
        name="conv_sample",
    )(cache_t, cache_t, u, conv_w.reshape(hist + 1, 1, ch), conv_b, ln_g, ln_b)


def _out_mlp_kernel(x_ref, a_ref, c_ref, wo_ref, bo_ref, g2_ref, wu_ref, wd_ref, gf_ref,
                    o_ref, hn_ref, *, final_norm):
    f = pl.program_id(1)

    @pl.when(f == 0)
    def _():
        mix = [ma + mc for ma, mc in zip(_dot_chunks(a_ref[...], wo_ref, slice(0, Q_DIM)),
                                         _dot_chunks(c_ref[...], wo_ref, slice(Q_DIM, None)))]
        h = x_ref[...] + jnp.concatenate(mix, axis=1) + bo_ref[...]
        o_ref[...] = h
        hn_ref[...] = _rms_norm_f32(h, g2_ref[...]).astype(hn_ref.dtype)

    up = jnp.concatenate(_dot_chunks(hn_ref[...], wu_ref), axis=1)
    act = jnp.square(jnp.maximum(up, 0.0)).astype(BF16)
    for c, down in enumerate(_dot_chunks(act, wd_ref)):
        o_ref[:, c * COL_CHUNK:(c + 1) * COL_CHUNK] += down

    if final_norm:
        @pl.when(f == pl.num_programs(1) - 1)
        def _():
            o_ref[...] = _rms_norm_f32(o_ref[...], gf_ref[...])


def _out_mlp(x, a, c, w_out, b_out, g2, w_up, w_down, gf, tm, tf, final_norm):
    m, d = x.shape
    conv_ch = w_out.shape[1] - Q_DIM
    ff = w_down.shape[1]
    row = lambda i, f: (i, 0)
    return pl.pallas_call(
        functools.partial(_out_mlp_kernel, final_norm=final_norm),
        grid=(m // tm, ff // tf),
        in_specs=[pl.BlockSpec((tm, d), row), pl.BlockSpec((tm, Q_DIM), row),
                  pl.BlockSpec((tm, conv_ch), row),
                  _resident(w_out.shape), _resident((1, d)), _resident((1, d)),
                  pl.BlockSpec((tf // COL_CHUNK, d, COL_CHUNK), lambda i, f: (f, 0, 0)),
                  pl.BlockSpec((w_down.shape[0], tf, COL_CHUNK), lambda i, f: (0, f, 0)),
                  _resident((1, d))],
        out_specs=pl.BlockSpec((tm, d), row),
        out_shape=jax.ShapeDtypeStruct((m, d), F32),
        scratch_shapes=[pltpu.VMEM((tm, d), BF16)],
        compiler_params=_params("arbitrary", "arbitrary"),
        name="out_mlp",
    )(x, a, c, w_out, b_out, g2, w_up, w_down, gf)


def _rope_tables(pos):
    half = ROT_DIM // 2
    inv_freq = jnp.power(jnp.float32(ROPE_THETA), -jnp.arange(half, dtype=jnp.float32) * 2.0 / ROT_DIM)
    ang = pos.astype(jnp.float32)[:, None] * inv_freq[None, :]
    cos, sin = jnp.cos(ang), jnp.sin(ang)
    t = pos.shape[0]
    pad = jnp.zeros((t, HEAD_DIM - ROT_DIM), F32)
    zeros = jnp.zeros((t, half), F32)
    per_head = lambda a, b, fill: jnp.concatenate([a, b, pad + fill], axis=1)
    reps = LANES // HEAD_DIM
    cos_t = jnp.tile(per_head(cos, cos, 1.0), (1, reps))
    sup_t = jnp.tile(per_head(zeros, sin, 0.0), (1, reps))
    sdn_t = jnp.tile(per_head(-sin, zeros, 0.0), (1, reps))
    return cos_t, sup_t, sdn_t


def _heads_to_igd(a, axis):
    shape = a.shape
    a = a.reshape(shape[:axis] + (N_KV_HEADS, GROUP, HEAD_DIM) + shape[axis + 1:])
    return jnp.swapaxes(a, axis, axis + 1).reshape(shape)


def _cast_slabs_kernel(w_ref, o_ref):
    o_ref[0] = w_ref[...].astype(o_ref.dtype)


def _cast_slabs(w, tk):
    k, n = w.shape
    return pl.pallas_call(
        _cast_slabs_kernel,
        grid=(n // COL_CHUNK, k // tk),
        in_specs=[pl.BlockSpec((tk, COL_CHUNK), lambda j, kk: (kk, j))],
        out_specs=pl.BlockSpec((1, tk, COL_CHUNK), lambda j, kk: (j, kk, 0)),
        out_shape=jax.ShapeDtypeStruct((n // COL_CHUNK, k, COL_CHUNK), BF16),
        compiler_params=_params("arbitrary", "arbitrary"),
        name="cast_slabs",
    )(w)


def _igd_source(idx):
    i, g, d = idx // KV_DIM, (idx // HEAD_DIM) % N_KV_HEADS, idx % HEAD_DIM
    return (g * GROUP + i) * HEAD_DIM + d


def _prep_w_in_kernel(w_ref, o_ref):
    w = w_ref[...].astype(o_ref.dtype)
    src = lax.broadcasted_iota(jnp.int32, (Q_DIM, Q_DIM), 0)
    dst = lax.broadcasted_iota(jnp.int32, (Q_DIM, Q_DIM), 1)
    move = jnp.where(src == _igd_source(dst), 1.0, 0.0).astype(o_ref.dtype)
    o_ref[:, 0:Q_DIM] = _dot(w[:, 0:Q_DIM], move).astype(o_ref.dtype)
    o_ref[:, Q_DIM:] = w[:, Q_DIM:]


def _prep_w_in(w, tk):
    k, n = w.shape
    return pl.pallas_call(
        _prep_w_in_kernel,
        grid=(k // tk,),
        in_specs=[pl.BlockSpec((tk, n), lambda j: (j, 0))],
        out_specs=pl.BlockSpec((tk, n), lambda j: (j, 0)),
        out_shape=jax.ShapeDtypeStruct((k, n), BF16),
        compiler_params=_params("arbitrary"),
        name="prep_w_in",
    )(w)


def _prep_w_out_kernel(w_ref, o_ref):
    w = w_ref[...].astype(o_ref.dtype)
    dst = lax.broadcasted_iota(jnp.int32, (Q_DIM, Q_DIM), 0)
    src = lax.broadcasted_iota(jnp.int32, (Q_DIM, Q_DIM), 1)
    move = jnp.where(src == _igd_source(dst), 1.0, 0.0).astype(o_ref.dtype)
    o_ref[0, 0:Q_DIM, :] = _dot(move, w[0:Q_DIM, :]).astype(o_ref.dtype)
    o_ref[0, Q_DIM:, :] = w[Q_DIM:, :]


def _prep_w_out(w):
    k, n = w.shape
    return pl.pallas_call(
        _prep_w_out_kernel,
        grid=(n // COL_CHUNK,),
        in_specs=[pl.BlockSpec((k, COL_CHUNK), lambda c: (0, c))],
        out_specs=pl.BlockSpec((1, k, COL_CHUNK), lambda c: (c, 0, 0)),
        out_shape=jax.ShapeDtypeStruct((n // COL_CHUNK, k, COL_CHUNK), BF16),
        compiler_params=_params("arbitrary"),
        name="prep_w_out",
    )(w)


def _tile(m, target):
    t = min(m, target)
    assert m % t == 0, (m, t)
    return t


def kernel(x_prompt, x_sample, cache_k, cache_v, cache_conv, norm1_g, w_in, b_in, attn_sinks,
           conv_w, conv_b, conv_ln_g, conv_ln_b, w_out, b_out, norm2_g, w_up, w_down, final_norm_g):
    batch, seq, d_model = x_prompt.shape
    nsamp, t_s, _ = x_sample.shape
    depth = w_in.shape[0]
    conv_ch = conv_w.shape[2]
    wb = cache_k.shape[2]
    hist = CONV_W - 1
    assert t_s == 1 and seq % WINDOW == 0 and wb == WINDOW and cache_conv.shape[2] == hist
    assert depth >= 1

    tabs_p = _rope_tables(jnp.arange(seq, dtype=jnp.int32))
    tabs_s1 = _rope_tables(PAST_LEN + jnp.arange(t_s, dtype=jnp.int32))
    tabs_s = tuple(jnp.broadcast_to(t, (nsamp, LANES)) for t in tabs_s1)
    tabs_sc = tuple(jnp.tile(t[0], KV_DIM // LANES).reshape(KV_DIM, 1) for t in tabs_s1)
    row = lambda a: a.reshape(1, -1)

    hp = x_prompt.reshape(batch * seq, d_model)
    hs = x_sample.reshape(nsamp * t_s, d_model)
    tm_in = _tile(seq, 512)
    tm_mlp = _tile(batch * seq, 512)
    tf_s = 2 * COL_CHUNK
    nseq = _tile(nsamp, 8)
    o1, o2 = Q_DIM, Q_DIM + 2 * KV_DIM
    pk, pv, pc, sk, sv, sc = [], [], [], [], [], []
    for l in range(depth):
        last = l == depth - 1
        w_in_l = _prep_w_in(w_in[l], _tile(d_model, 512))
        b_in_l = row(jnp.concatenate([_heads_to_igd(b_in[l][:o1], 0), b_in[l][o1:]]))
        w_out_l = _prep_w_out(w_out[l])
        w_up_l, w_down_l = _cast_slabs(w_up[l], _tile(d_model, 2048)), _cast_slabs(w_down[l], 2048)
        sinks = attn_sinks[l].astype(F32)
        sink_rows = sinks.reshape(N_KV_HEADS, GROUP).T.reshape(N_HEADS, 1)
        g1, g2, gf = row(norm1_g[l]), row(norm2_g[l]), row(final_norm_g)
        cb, lg, lb, bo = row(conv_b[l]), row(conv_ln_g[l]), row(conv_ln_b[l]), row(b_out[l])

        q, k, v, u, pk_l, pv_l, pc_l = _in_proj(hp, g1, w_in_l, b_in_l, *tabs_p, tm_in, batch)
        by_part = lambda a: a.reshape(a.shape[0], PARTS, KV_DIM).transpose(1, 0, 2)
        hp = _mlp_mix(hp, q, k, v, u, sinks, by_part(conv_w[l]), by_part(cb), lg, lb, w_out_l, bo, g2, gf,
                      w_up_l, w_down_l, seq, tm_mlp, 1, last)
        pk.append(pk_l.reshape(batch, WINDOW, N_KV_HEADS, HEAD_DIM))
        pv.append(pv_l.reshape(batch, WINDOW, N_KV_HEADS, HEAD_DIM))
        pc.append(pc_l)

        cache_kt = jnp.transpose(cache_k[l], (0, 2, 3, 1)).reshape(nsamp, KV_DIM, wb)
        cache_vt = jnp.transpose(cache_v[l], (0, 2, 3, 1)).reshape(nsamp, KV_DIM, wb)
        cache_ct = jnp.transpose(cache_conv[l], (1, 0, 2))
        bkv_c = b_in[l][o1:o2].reshape(2 * KV_DIM, 1)
        q, kt, vt, u = _in_proj_sample(hs, g1, w_in_l, b_in_l, bkv_c, tabs_s, tabs_sc)
        q_rep = jnp.repeat(q.reshape(nsamp, GROUP, KV_DIM), N_KV_HEADS, axis=1)
        by_block = lambda a: a.reshape(KV_DIM, nsamp // nseq, nseq).transpose(1, 0, 2)
        a8, nkt, nvt = _attn_sample(q_rep, by_block(kt), by_block(vt), cache_kt, cache_vt, sink_rows, nseq)
        c_o, nct = _conv_sample(cache_ct, u, conv_w[l], cb, lg, lb)
        a_o = a8[:, :GROUP, :].reshape(nsamp, Q_DIM).astype(BF16)
        hs = _out_mlp(hs, a_o, c_o, w_out_l, bo, g2, w_up_l, w_down_l, gf, nsamp, tf_s, last)
        to_cache = lambda a: jnp.transpose(a.reshape(nsamp, N_KV_HEADS, HEAD_DIM, wb), (0, 3, 1, 2))
        sk.append(to_cache(nkt))
        sv.append(to_cache(nvt))
        sc.append(jnp.transpose(nct, (1, 0, 2)))

    y_prompt = hp.reshape(batch, seq, d_model)
    y_sample = hs.reshape(nsamp, t_s, d_model)
    return (y_prompt, y_sample, jnp.stack(pk), jnp.stack(pv), jnp.stack(pc),
            jnp.stack(sk), jnp.stack(sv), jnp.stack(sc))
```

```python
import functools

import jax
import jax.numpy as jnp
import numpy as np
from jax import lax
from jax.experimental import pallas as pl
from jax.experimental.pallas import tpu as pltpu

HEAD_DIM = 64
N_HEADS = 16
N_KV_HEADS = 4
GROUP = N_HEADS // N_KV_HEADS
WINDOW = 128
ROT_DIM = HEAD_DIM // 4
ROPE_THETA = 500000.0
ATTN_SCALE = HEAD_DIM ** -0.5
Q_DIM = N_HEADS * HEAD_DIM
KV_DIM = N_KV_HEADS * HEAD_DIM
CONV_W = 31
PAST_LEN = 16384
EPS = 1e-5

LANES = 128
SUBLANES = 8
CONV_HALO = 32
CONV_ROWS = 64
VMEM_LIMIT_BYTES = 60 * 1024 * 1024
PARTS = GROUP
COL_CHUNK = 512

BF16 = jnp.bfloat16
F32 = jnp.float32


def _resident(shape):
    return pl.BlockSpec(shape, lambda *_: (0,) * len(shape), pipeline_mode=pl.Buffered(1))


def _params(*semantics):
    return pltpu.CompilerParams(dimension_semantics=semantics, vmem_limit_bytes=VMEM_LIMIT_BYTES)


def _rms_norm_f32(x, g):
    return x * lax.rsqrt(jnp.mean(x * x, axis=-1, keepdims=True) + EPS) * g


def _layer_norm_swish(y, g, b):
    mu = jnp.mean(y, axis=-1, keepdims=True)
    yc = y - mu
    var = jnp.mean(yc * yc, axis=-1, keepdims=True)
    yn = yc * lax.rsqrt(var + EPS) * g + b
    return yn * jax.nn.sigmoid(yn)


def _dot(a, b):
    return jnp.dot(a, b, preferred_element_type=F32)


def _dot_nt(a, b):
    return lax.dot_general(a, b, (((1,), (1,)), ((), ())), preferred_element_type=F32)


def _dot_chunks(a, w_ref, rows=slice(None)):
    return [_dot(a, w_ref[c, rows, :]) for c in range(w_ref.shape[0])]


def _sink_softmax(s, sink):
    m = jnp.maximum(jnp.max(s, axis=-1, keepdims=True), sink)
    p = jnp.exp(s - m)
    return p / (jnp.sum(p, axis=-1, keepdims=True) + jnp.exp(sink - m))


def _rope(z, cos, sin_up, sin_dn, axis):
    n = z.shape[axis]
    half = ROT_DIM // 2
    return z * cos + pltpu.roll(z, half, axis) * sin_up + pltpu.roll(z, n - half, axis) * sin_dn


def _rope_lanes(z, cos, sin_up, sin_dn):
    cols = [_rope(z[:, c * LANES:(c + 1) * LANES], cos, sin_up, sin_dn, 1) for c in range(z.shape[1] // LANES)]
    return jnp.concatenate(cols, axis=1)


def _in_proj_kernel(x_ref, g_ref, w_ref, b_ref, cos_ref, sup_ref, sdn_ref,
                    q_ref, k_ref, v_ref, u_ref, pk_ref, pv_ref, pc_ref):
    conv_ch = u_ref.shape[0] * u_ref.shape[2]
    tm = x_ref.shape[0]
    hist = pc_ref.shape[1]
    o1, o2, o3, o4 = Q_DIM, Q_DIM + KV_DIM, Q_DIM + 2 * KV_DIM, Q_DIM + 2 * KV_DIM + conv_ch
    hn = _rms_norm_f32(x_ref[...], g_ref[...]).astype(BF16)
    cos, sup, sdn = cos_ref[...], sup_ref[...], sdn_ref[...]

    zq = _dot(hn, w_ref[:, 0:o1]) + b_ref[:, 0:o1]
    q = (_rope_lanes(zq, cos, sup, sdn) * ATTN_SCALE).astype(q_ref.dtype)
    zk = _dot(hn, w_ref[:, o1:o2]) + b_ref[:, o1:o2]
    k_ref[...] = _rope_lanes(zk, cos, sup, sdn)
    v_ref[...] = _dot(hn, w_ref[:, o2:o3]) + b_ref[:, o2:o3]
    zu = _dot(hn, w_ref[:, o3:o4]) + b_ref[:, o3:o4]
    zg = _dot(hn, w_ref[:, o4:]) + b_ref[:, o4:]
    u = zu * jax.nn.sigmoid(zg)
    for p in range(PARTS):
        q_ref[p] = q[:, p * KV_DIM:(p + 1) * KV_DIM]
        u_ref[p] = u[:, p * KV_DIM:(p + 1) * KV_DIM]
    pk_ref[0] = k_ref[tm - WINDOW:, :]
    pv_ref[0] = v_ref[tm - WINDOW:, :]
    pc_ref[0] = u[tm - hist:, :]


def _in_proj(x, g, w, b, cos, sup, sdn, tm, batch):
    m, d = x.shape
    n = w.shape[1]
    conv_ch = (n - Q_DIM - 2 * KV_DIM) // 2
    nt = cos.shape[0] // tm
    hist = CONV_W - 1
    row = lambda i: (i, 0)
    parts = lambda i: (0, i, 0)
    tab = lambda i: (i % nt, 0)
    seq = lambda i: (i // nt, 0, 0)
    assert conv_ch == PARTS * KV_DIM and Q_DIM == PARTS * KV_DIM
    return pl.pallas_call(
        _in_proj_kernel,
        grid=(m // tm,),
        in_specs=[pl.BlockSpec((tm, d), row), _resident((1, d)), _resident((d, n)), _resident((1, n)),
                  pl.BlockSpec((tm, LANES), tab), pl.BlockSpec((tm, LANES), tab),
                  pl.BlockSpec((tm, LANES), tab)],
        out_specs=[pl.BlockSpec((PARTS, tm, KV_DIM), parts), pl.BlockSpec((tm, KV_DIM), row),
                   pl.BlockSpec((tm, KV_DIM), row), pl.BlockSpec((PARTS, tm, KV_DIM), parts),
                   pl.BlockSpec((1, WINDOW, KV_DIM), seq), pl.BlockSpec((1, WINDOW, KV_DIM), seq),
                   pl.BlockSpec((1, hist, conv_ch), seq)],
        out_shape=[jax.ShapeDtypeStruct((PARTS, m, KV_DIM), BF16), jax.ShapeDtypeStruct((m, KV_DIM), F32),
                   jax.ShapeDtypeStruct((m, KV_DIM), F32), jax.ShapeDtypeStruct((PARTS, m, KV_DIM), F32),
                   jax.ShapeDtypeStruct((batch, WINDOW, KV_DIM), F32),
                   jax.ShapeDtypeStruct((batch, WINDOW, KV_DIM), F32),
                   jax.ShapeDtypeStruct((batch, hist, conv_ch), F32)],
        compiler_params=_params("arbitrary"),
        name="in_proj",
    )(x, g, w, b, cos, sup, sdn)


def _attn_scores(q, k_prev, k_own, v_prev, v_own, has_prev):
    blk = q.shape[0]
    kk = jnp.concatenate([k_prev, k_own], axis=0).astype(BF16)
    vv = jnp.concatenate([v_prev, v_own], axis=0).astype(BF16)
    lane_group = lax.broadcasted_iota(jnp.int32, kk.shape, 1) // HEAD_DIM
    zero = jnp.zeros_like(kk)
    kstack = jnp.concatenate([jnp.where(lane_group == g, kk, zero) for g in range(N_KV_HEADS)], axis=0)
    vstack = jnp.concatenate([jnp.where(lane_group == g, vv, zero) for g in range(N_KV_HEADS)], axis=0)
    qi = lax.broadcasted_iota(jnp.int32, (blk, 2 * blk), 0)
    kj = lax.broadcasted_iota(jnp.int32, (blk, 2 * blk), 1)
    rel = qi + blk - kj
    valid = (rel >= 0) & (rel < WINDOW) & (has_prev | (kj >= blk))
    return _dot_nt(q, kstack), valid, vstack


def _attn_probs(s_all, valid, sink_ref, part, zero_tile):
    width = valid.shape[1]
    probs = []
    for g in range(N_KV_HEADS):
        s = jnp.where(valid, s_all[:, g * width:(g + 1) * width], -jnp.inf)
        p = _sink_softmax(s, sink_ref[g * GROUP + part])
        if g == 0:
            top = jnp.concatenate([p[0:SUBLANES, 0:LANES] + zero_tile, p[0:SUBLANES, LANES:]], axis=1)
            p = jnp.concatenate([top, p[SUBLANES:, :]], axis=0)
        probs.append(p.astype(BF16))
    return jnp.concatenate(probs, axis=1)


def _conv_stage(u_own, halo, ue_ref, has_prev):
    ue_ref[0:CONV_HALO, :] = jnp.where(has_prev, halo, jnp.zeros_like(halo))
    ue_ref[CONV_HALO:, :] = u_own


def _conv_lanes(c, cw_ref, cb_ref, ue_ref, c_out):
    blk = ue_ref.shape[-2] - CONV_HALO
    first = CONV_HALO - (CONV_W - 1)
    cs = slice(c * LANES, (c + 1) * LANES)
    seen = None
    for s0 in range(0, blk, CONV_ROWS):
        win = ue_ref[s0:s0 + CONV_ROWS + CONV_HALO, cs]
        out = cb_ref[:, cs]
        for r in range(SUBLANES):
            rows = CONV_ROWS + (SUBLANES if r else 0)
            part = None
            for a in range((first + CONV_W - 1 - r) // SUBLANES + 1):
                t = a * SUBLANES + r - first
                if 0 <= t < CONV_W:
                    term = cw_ref[t:t + 1, cs] * win[a * SUBLANES:a * SUBLANES + rows, :]
                    part = term if part is None else part + term
            out = out + part[r:r + CONV_ROWS, :]
        c_out[s0:s0 + CONV_ROWS, cs] = out
        for r0 in range(0, CONV_ROWS, SUBLANES):
            tile = out[r0:r0 + SUBLANES, :]
            seen = tile if seen is None else jnp.maximum(seen, tile)
    return seen


def _mlp_mix_kernel(x_ref, q_ref, k_ref, kp_ref, v_ref, vp_ref, u_ref, uh_ref, sink_ref, cw_ref, cb_ref,
                    lg_ref, lb_ref, wo_ref, bo_ref, g2_ref, gf_ref, wu_ref, wd_ref,
                    o_ref, hn_ref, a_s, c_s, ue_ref, *, nb, nrow, final_norm):
    i, f = pl.program_id(0), pl.program_id(1)
    blk = WINDOW
    blocks_per_tile = x_ref.shape[0] // blk
    parts_per_step = ue_ref.shape[0]
    steps_per_block = PARTS // parts_per_step

    @pl.when((i == 0) & (f == 0))
    def _():
        a_s[...] = jnp.zeros_like(a_s)
        c_s[...] = jnp.zeros_like(c_s)

    @pl.when(f == 0)
    def _():
        a = jnp.concatenate([a_s[p] for p in range(PARTS)], axis=1)
        conv = jnp.concatenate([c_s[p] for p in range(PARTS)], axis=1)
        c = _layer_norm_swish(conv, lg_ref[...], lb_ref[...]).astype(BF16)
        mix = [ma + mc for ma, mc in zip(_dot_chunks(a, wo_ref, slice(0, Q_DIM)),
                                         _dot_chunks(c, wo_ref, slice(Q_DIM, None)))]
        h = x_ref[...] + jnp.concatenate(mix, axis=1) + bo_ref[...]
        o_ref[...] = h
        hn_ref[...] = _rms_norm_f32(h, g2_ref[...]).astype(hn_ref.dtype)

    sub, part0 = f // steps_per_block, (f % steps_per_block) * parts_per_step
    parts = [part0 + j for j in range(parts_per_step)]
    gblk = jnp.minimum(i, nrow - 1) * blocks_per_tile + sub
    has_prev = gblk % nb != 0
    rows = pl.ds(pl.multiple_of(sub * blk, blk), blk)
    prev_rows = pl.ds(pl.multiple_of(jnp.maximum(sub - 1, 0) * blk, blk), blk)
    halo_rows = pl.ds(pl.multiple_of(jnp.maximum(sub * blk - CONV_HALO, 0), CONV_HALO), CONV_HALO)
    k_prev = jnp.where(sub == 0, kp_ref[...], k_ref[prev_rows, :])
    v_prev = jnp.where(sub == 0, vp_ref[...], v_ref[prev_rows, :])
    conv_done = []
    for j, part in enumerate(parts):
        halo = jnp.where(sub == 0, uh_ref[part], u_ref[part, halo_rows, :])
        ue, conv_out = ue_ref.at[j], c_s.at[part, rows, :]
        _conv_stage(u_ref[part, rows, :], halo, ue, has_prev)
        seen = jnp.maximum(_conv_lanes(0, cw_ref.at[part], cb_ref.at[part], ue, conv_out),
                           _conv_lanes(1, cw_ref.at[part], cb_ref.at[part], ue, conv_out))
        conv_done.append(jnp.minimum(jnp.abs(seen), 0.0))
    scores = [_attn_scores(q_ref[part, rows, :], k_prev, k_ref[rows, :], v_prev, v_ref[rows, :], has_prev)
              for part in parts]
    up = jnp.concatenate(_dot_chunks(hn_ref[...], wu_ref), axis=1)
    act = jnp.square(jnp.maximum(up, 0.0)).astype(BF16)
    for part, (s_all, valid, vstack), zero in zip(parts, scores, conv_done):
        probs = _attn_probs(s_all, valid, sink_ref, part, zero)
        a_s[part, rows, :] = _dot(probs, vstack).astype(a_s.dtype)
    for c, down in enumerate(_dot_chunks(act, wd_ref)):
        o_ref[:, c * COL_CHUNK:(c + 1) * COL_CHUNK] += down

    if final_norm:
        @pl.when(f == pl.num_programs(1) - 1)
        def _():
            o_ref[...] = _rms_norm_f32(o_ref[...], gf_ref[...])


def _mlp_mix(x, q, k, v, u, sinks, conv_w, conv_b, ln_g, ln_b, w_out, b_out, g2, gf, w_up, w_down,
             seq, tm, parts_per_step, final_norm):
    m, d = x.shape
    conv_ch = u.shape[0] * u.shape[2]
    ff = w_down.shape[1]
    blk = WINDOW
    nb = seq // blk
    nrow = m // tm
    bpt = tm // blk
    nf = bpt * PARTS // parts_per_step
    assert ff % (nf * COL_CHUNK) == 0 and conv_ch == PARTS * KV_DIM and Q_DIM == PARTS * KV_DIM
    tf = ff // nf
    hpb = blk // CONV_HALO
    mix_row = lambda i: jnp.minimum(i, nrow - 1)
    mlp_tile = lambda i, f: (jnp.maximum(i - 1, 0), 0)
    mix_tile = lambda i, f: (mix_row(i), 0)
    mix_parts = lambda i, f: (0, mix_row(i), 0)
    prev = lambda i, f: (jnp.maximum(mix_row(i) * bpt - 1, 0), 0)
    halo_parts = lambda i, f: (0, jnp.maximum(mix_row(i) * bpt * hpb - 1, 0), 0)
    return pl.pallas_call(
        functools.partial(_mlp_mix_kernel, nb=nb, nrow=nrow, final_norm=final_norm),
        grid=(nrow + 1, nf),
        in_specs=[pl.BlockSpec((tm, d), mlp_tile),
                  pl.BlockSpec((PARTS, tm, KV_DIM), mix_parts),
                  pl.BlockSpec((tm, KV_DIM), mix_tile), pl.BlockSpec((blk, KV_DIM), prev),
                  pl.BlockSpec((tm, KV_DIM), mix_tile), pl.BlockSpec((blk, KV_DIM), prev),
                  pl.BlockSpec((PARTS, tm, KV_DIM), mix_parts),
                  pl.BlockSpec((PARTS, CONV_HALO, KV_DIM), halo_parts),
                  pl.BlockSpec(memory_space=pltpu.SMEM),
                  _resident((PARTS, CONV_W, KV_DIM)), _resident((PARTS, 1, KV_DIM)),
                  _resident((1, conv_ch)), _resident((1, conv_ch)),
                  _resident(w_out.shape), _resident((1, d)), _resident((1, d)), _resident((1, d)),
                  pl.BlockSpec((tf // COL_CHUNK, d, COL_CHUNK), lambda i, f: (f, 0, 0)),
                  pl.BlockSpec((w_down.shape[0], tf, COL_CHUNK), lambda i, f: (0, f, 0))],
        out_specs=pl.BlockSpec((tm, d), mlp_tile),
        out_shape=jax.ShapeDtypeStruct((m, d), F32),
        scratch_shapes=[pltpu.VMEM((tm, d), BF16),
                        pltpu.VMEM((PARTS, tm, KV_DIM), BF16),
                        pltpu.VMEM((PARTS, tm, KV_DIM), F32),
                        pltpu.VMEM((parts_per_step, CONV_HALO + blk, KV_DIM), F32)],
        compiler_params=_params("arbitrary", "arbitrary"),
        name="mlp_mix",
    )(x, q, k, k, v, v, u, u, sinks, conv_w, conv_b, ln_g, ln_b, w_out, b_out, g2, gf, w_up, w_down)


def _in_proj_sample_kernel(x_ref, g_ref, w_ref, b_ref, bkv_ref,
                           cos_ref, sup_ref, sdn_ref, cosc_ref, supc_ref, sdnc_ref,
                           q_ref, kt_ref, vt_ref, u_ref):
    conv_ch = u_ref.shape[1]
    o2, o3 = Q_DIM + 2 * KV_DIM, Q_DIM + 2 * KV_DIM + conv_ch
    hn = _rms_norm_f32(x_ref[...], g_ref[...]).astype(BF16)
    zq = _dot(hn, w_ref[:, 0:Q_DIM]) + b_ref[:, 0:Q_DIM]
    q_ref[...] = _rope_lanes(zq, cos_ref[...], sup_ref[...], sdn_ref[...]) * ATTN_SCALE
    zkv = lax.dot_general(w_ref[:, Q_DIM:o2], hn, (((0,), (1,)), ((), ())), preferred_element_type=F32) + bkv_ref[...]
    kt_ref[...] = _rope(zkv[0:KV_DIM, :], cosc_ref[...], supc_ref[...], sdnc_ref[...], 0)
    vt_ref[...] = zkv[KV_DIM:, :]
    zu = _dot(hn, w_ref[:, o2:o3]) + b_ref[:, o2:o3]
    zg = _dot(hn, w_ref[:, o3:]) + b_ref[:, o3:]
    u_ref[...] = zu * jax.nn.sigmoid(zg)


def _in_proj_sample(x, g, w, b, bkv_c, row_tabs, col_tabs):
    n, d = x.shape
    conv_ch = (w.shape[1] - Q_DIM - 2 * KV_DIM) // 2
    full = lambda a: _resident(a.shape)
    args = (x, g, w, b, bkv_c, *row_tabs, *col_tabs)
    return pl.pallas_call(
        _in_proj_sample_kernel,
        grid=(1,),
        in_specs=[full(a) for a in args],
        out_specs=[_resident((n, Q_DIM)), _resident((KV_DIM, n)), _resident((KV_DIM, n)), _resident((n, conv_ch))],
        out_shape=[jax.ShapeDtypeStruct((n, Q_DIM), F32), jax.ShapeDtypeStruct((KV_DIM, n), F32),
                   jax.ShapeDtypeStruct((KV_DIM, n), F32), jax.ShapeDtypeStruct((n, conv_ch), F32)],
        compiler_params=_params("arbitrary"),
        name="in_proj_sample",
    )(*args)


def _attn_sample_kernel(q_ref, kn_ref, vn_ref, ck_ref, cv_ref, sink_ref, ao_ref, nk_ref, nv_ref):
    nseq, _, wb = ck_ref.shape
    rows = N_HEADS
    lane_group = lax.broadcasted_iota(jnp.int32, (rows, KV_DIM), 1) // HEAD_DIM
    row_group = lax.broadcasted_iota(jnp.int32, (rows, KV_DIM), 0) % N_KV_HEADS
    own_lanes = lane_group == row_group
    fold = (lax.broadcasted_iota(jnp.int32, (SUBLANES, rows), 1) // N_KV_HEADS
            == lax.broadcasted_iota(jnp.int32, (SUBLANES, rows), 0)).astype(F32)
    newest = lax.broadcasted_iota(jnp.int32, (KV_DIM, wb), 1) == wb - 1
    sink = sink_ref[...]
    kn, vn = kn_ref[0], vn_ref[0]

    seqs = range(nseq)
    knew = [jnp.where(newest, kn[:, n:n + 1], pltpu.roll(ck_ref[n], wb - 1, 1)) for n in seqs]
    vnew = [jnp.where(newest, vn[:, n:n + 1], pltpu.roll(cv_ref[n], wb - 1, 1)) for n in seqs]
    for n in seqs:
        nk_ref[n] = knew[n]
        nv_ref[n] = vnew[n]
    qrows = [jnp.where(own_lanes, q_ref[n], jnp.zeros((rows, KV_DIM), F32)).astype(BF16) for n in seqs]
    s = [_dot(qrows[n], knew[n].astype(BF16)) for n in seqs]
    p = [_sink_softmax(s[n], sink).astype(BF16) for n in seqs]
    o = [_dot_nt(p[n], vnew[n].astype(BF16)) for n in seqs]
    for n in seqs:
        ao_ref[n] = _dot(fold, jnp.where(own_lanes, o[n], jnp.zeros_like(o[n])))


def _attn_sample(q_rep, kn, vn, cache_kt, cache_vt, sink_rows, nseq):
    n, _, wb = cache_kt.shape
    r3 = lambda s: (s, 0, 0)
    return pl.pallas_call(
        _attn_sample_kernel,
        grid=(n // nseq,),
        in_specs=[pl.BlockSpec((nseq, N_HEADS, KV_DIM), r3),
                  pl.BlockSpec((1, KV_DIM, nseq), r3), pl.BlockSpec((1, KV_DIM, nseq), r3),
                  pl.BlockSpec((nseq, KV_DIM, wb), r3), pl.BlockSpec((nseq, KV_DIM, wb), r3),
                  _resident((N_HEADS, 1))],
        out_specs=[pl.BlockSpec((nseq, SUBLANES, KV_DIM), r3),
                   pl.BlockSpec((nseq, KV_DIM, wb), r3), pl.BlockSpec((nseq, KV_DIM, wb), r3)],
        out_shape=[jax.ShapeDtypeStruct((n, SUBLANES, KV_DIM), F32),
                   jax.ShapeDtypeStruct((n, KV_DIM, wb), F32), jax.ShapeDtypeStruct((n, KV_DIM, wb), F32)],
        compiler_params=_params("arbitrary"),
        name="attn_sample",
    )(q_rep, kn, vn, cache_kt, cache_vt, sink_rows)


def _conv_sample_kernel(cc_ref, cn_ref, u_ref, cw_ref, cb_ref, lg_ref, lb_ref, co_ref, nc_ref, acc_ref, *, hist):
    j = pl.program_id(0)
    u = u_ref[...]

    @pl.when(j == 0)
    def _():
        acc_ref[...] = jnp.zeros_like(acc_ref)

    acc_ref[...] += cw_ref[j] * cc_ref[0]

    @pl.when(j < hist - 1)
    def _():
        nc_ref[0] = cn_ref[0]

    @pl.when(j == hist - 1)
    def _():
        nc_ref[0] = u
        y = acc_ref[...] + cw_ref[hist] * u + cb_ref[...]
        co_ref[...] = _layer_norm_swish(y, lg_ref[...], lb_ref[...]).astype(co_ref.dtype)


def _conv_sample(cache_t, u, conv_w, conv_b, ln_g, ln_b):
    hist, n, ch = cache_t.shape
    tap = lambda j: (j, 0, 0)
    return pl.pallas_call(
        functools.partial(_conv_sample_kernel, hist=hist),
        grid=(hist,),
        in_specs=[pl.BlockSpec((1, n, ch), tap),
                  pl.BlockSpec((1, n, ch), lambda j: (jnp.minimum(j + 1, hist - 1), 0, 0)),
                  _resident((n, ch)), _resident((hist + 1, 1, ch)),
                  _resident((1, ch)), _resident((1, ch)), _resident((1, ch))],
        out_specs=[_resident((n, ch)), pl.BlockSpec((1, n, ch), tap)],
        out_shape=[jax.ShapeDtypeStruct((n, ch), BF16), jax.ShapeDtypeStruct((hist, n, ch), F32)],
        scratch_shapes=[pltpu.VMEM((n, ch), F32)],
        compiler_params=_params("arbitrary"),
        name="conv_sample",
    )(cache_t, cache_t, u, conv_w.reshape(hist + 1, 1, ch), conv_b, ln_g, ln_b)


def _out_mlp_kernel(x_ref, a_ref, c_ref, wo_ref, bo_ref, g2_ref, wu_ref, wd_ref, gf_ref,
                    o_ref, hn_ref, *, final_norm):
    f = pl.program_id(1)

    @pl.when(f == 0)
    def _():
        mix = [ma + mc for ma, mc in zip(_dot_chunks(a_ref[...], wo_ref, slice(0, Q_DIM)),
                                         _dot_chunks(c_ref[...], wo_ref, slice(Q_DIM, None)))]
        h = x_ref[...] + jnp.concatenate(mix, axis=1) + bo_ref[...]
        o_ref[...] = h
        hn_ref[...] = _rms_norm_f32(h, g2_ref[...]).astype(hn_ref.dtype)

    up = jnp.concatenate(_dot_chunks(hn_ref[...], wu_ref), axis=1)
    act = jnp.square(jnp.maximum(up, 0.0)).astype(BF16)
    for c, down in enumerate(_dot_chunks(act, wd_ref)):
        o_ref[:, c * COL_CHUNK:(c + 1) * COL_CHUNK] += down

    if final_norm:
        @pl.when(f == pl.num_programs(1) - 1)
        def _():
            o_ref[...] = _rms_norm_f32(o_ref[...], gf_ref[...])


def _out_mlp(x, a, c, w_out, b_out, g2, w_up, w_down, gf, tm, tf, final_norm):
    m, d = x.shape
    conv_ch = w_out.shape[1] - Q_DIM
    ff = w_down.shape[1]
    row = lambda i, f: (i, 0)
    return pl.pallas_call(
        functools.partial(_out_mlp_kernel, final_norm=final_norm),
        grid=(m // tm, ff // tf),
        in_specs=[pl.BlockSpec((tm, d), row), pl.BlockSpec((tm, Q_DIM), row),
                  pl.BlockSpec((tm, conv_ch), row),
                  _resident(w_out.shape), _resident((1, d)), _resident((1, d)),
                  pl.BlockSpec((tf // COL_CHUNK, d, COL_CHUNK), lambda i, f: (f, 0, 0)),
                  pl.BlockSpec((w_down.shape[0], tf, COL_CHUNK), lambda i, f: (0, f, 0)),
                  _resident((1, d))],
        out_specs=pl.BlockSpec((tm, d), row),
        out_shape=jax.ShapeDtypeStruct((m, d), F32),
        scratch_shapes=[pltpu.VMEM((tm, d), BF16)],
        compiler_params=_params("arbitrary", "arbitrary"),
        name="out_mlp",
    )(x, a, c, w_out, b_out, g2, w_up, w_down, gf)


def _rope_tables(pos):
    half = ROT_DIM // 2
    inv_freq = jnp.power(jnp.float32(ROPE_THETA), -jnp.arange(half, dtype=jnp.float32) * 2.0 / ROT_DIM)
    ang = pos.astype(jnp.float32)[:, None] * inv_freq[None, :]
    cos, sin = jnp.cos(ang), jnp.sin(ang)
    t = pos.shape[0]
    pad = jnp.zeros((t, HEAD_DIM - ROT_DIM), F32)
    zeros = jnp.zeros((t, half), F32)
    per_head = lambda a, b, fill: jnp.concatenate([a, b, pad + fill], axis=1)
    reps = LANES // HEAD_DIM
    cos_t = jnp.tile(per_head(cos, cos, 1.0), (1, reps))
    sup_t = jnp.tile(per_head(zeros, sin, 0.0), (1, reps))
    sdn_t = jnp.tile(per_head(-sin, zeros, 0.0), (1, reps))
    return cos_t, sup_t, sdn_t


def _heads_to_igd(a, axis):
    shape = a.shape
    a = a.reshape(shape[:axis] + (N_KV_HEADS, GROUP, HEAD_DIM) + shape[axis + 1:])
    return jnp.swapaxes(a, axis, axis + 1).reshape(shape)


def _cast_slabs_kernel(w_ref, o_ref):
    o_ref[0] = w_ref[...].astype(o_ref.dtype)


def _cast_slabs(w, tk):
    k, n = w.shape
    return pl.pallas_call(
        _cast_slabs_kernel,
        grid=(n // COL_CHUNK, k // tk),
        in_specs=[pl.BlockSpec((tk, COL_CHUNK), lambda j, kk: (kk, j))],
        out_specs=pl.BlockSpec((1, tk, COL_CHUNK), lambda j, kk: (j, kk, 0)),
        out_shape=jax.ShapeDtypeStruct((n // COL_CHUNK, k, COL_CHUNK), BF16),
        compiler_params=_params("arbitrary", "arbitrary"),
        name="cast_slabs",
    )(w)


def _igd_source(idx):
    i, g, d = idx // KV_DIM, (idx // HEAD_DIM) % N_KV_HEADS, idx % HEAD_DIM
    return (g * GROUP + i) * HEAD_DIM + d


def _prep_w_in_kernel(w_ref, o_ref):
    w = w_ref[...].astype(o_ref.dtype)
    src = lax.broadcasted_iota(jnp.int32, (Q_DIM, Q_DIM), 0)
    dst = lax.broadcasted_iota(jnp.int32, (Q_DIM, Q_DIM), 1)
    move = jnp.where(src == _igd_source(dst), 1.0, 0.0).astype(o_ref.dtype)
    o_ref[:, 0:Q_DIM] = _dot(w[:, 0:Q_DIM], move).astype(o_ref.dtype)
    o_ref[:, Q_DIM:] = w[:, Q_DIM:]


def _prep_w_in(w, tk):
    k, n = w.shape
    return pl.pallas_call(
        _prep_w_in_kernel,
        grid=(k // tk,),
        in_specs=[pl.BlockSpec((tk, n), lambda j: (j, 0))],
        out_specs=pl.BlockSpec((tk, n), lambda j: (j, 0)),
        out_shape=jax.ShapeDtypeStruct((k, n), BF16),
        compiler_params=_params("arbitrary"),
        name="prep_w_in",
    )(w)


def _prep_w_out_kernel(w_ref, o_ref):
    w = w_ref[...].astype(o_ref.dtype)
    dst = lax.broadcasted_iota(jnp.int32, (Q_DIM, Q_DIM), 0)
    src = lax.broadcasted_iota(jnp.int32, (Q_DIM, Q_DIM), 1)
    move = jnp.where(src == _igd_source(dst), 1.0, 0.0).astype(o_ref.dtype)
    o_ref[0, 0:Q_DIM, :] = _dot(move, w[0:Q_DIM, :]).astype(o_ref.dtype)
    o_ref[0, Q_DIM:, :] = w[Q_DIM:, :]


def _prep_w_out(w):
    k, n = w.shape
    return pl.pallas_call(
        _prep_w_out_kernel,
        grid=(n // COL_CHUNK,),
        in_specs=[pl.BlockSpec((k, COL_CHUNK), lambda c: (0, c))],
        out_specs=pl.BlockSpec((1, k, COL_CHUNK), lambda c: (c, 0, 0)),
        out_shape=jax.ShapeDtypeStruct((n // COL_CHUNK, k, COL_CHUNK), BF16),
        compiler_params=_params("arbitrary"),
        name="prep_w_out",
    )(w)


def _tile(m, target):
    t = min(m, target)
    assert m % t == 0, (m, t)
    return t


def kernel(x_prompt, x_sample, cache_k, cache_v, cache_conv, norm1_g, w_in, b_in, attn_sinks,
           conv_w, conv_b, conv_ln_g, conv_ln_b, w_out, b_out, norm2_g, w_up, w_down, final_norm_g):
    batch, seq, d_model = x_prompt.shape
    nsamp, t_s, _ = x_sample.shape
    depth = w_in.shape[0]
    conv_ch = conv_w.shape[2]
    wb = cache_k.shape[2]
    hist = CONV_W - 1
    assert t_s == 1 and seq % WINDOW == 0 and wb == WINDOW and cache_conv.shape[2] == hist
    assert depth >= 1

    tabs_p = _rope_tables(jnp.arange(seq, dtype=jnp.int32))
    tabs_s1 = _rope_tables(PAST_LEN + jnp.arange(t_s, dtype=jnp.int32))
    tabs_s = tuple(jnp.broadcast_to(t, (nsamp, LANES)) for t in tabs_s1)
    tabs_sc = tuple(jnp.tile(t[0], KV_DIM // LANES).reshape(KV_DIM, 1) for t in tabs_s1)
    row = lambda a: a.reshape(1, -1)

    hp = x_prompt.reshape(batch * seq, d_model)
    hs = x_sample.reshape(nsamp * t_s, d_model)
    tm_in = _tile(seq, 512)
    tm_mlp = _tile(batch * seq, 512)
    tf_s = 2 * COL_CHUNK
    nseq = _tile(nsamp, 8)
    o1, o2 = Q_DIM, Q_DIM + 2 * KV_DIM
    pk, pv, pc, sk, sv, sc = [], [], [], [], [], []
    for l in range(depth):
        last = l == depth - 1
        w_in_l = _prep_w_in(w_in[l], _tile(d_model, 512))
        b_in_l = row(jnp.concatenate([_heads_to_igd(b_in[l][:o1], 0), b_in[l][o1:]]))
        w_out_l = _prep_w_out(w_out[l])
        w_up_l, w_down_l = _cast_slabs(w_up[l], _tile(d_model, 2048)), _cast_slabs(w_down[l], 2048)
        sinks = attn_sinks[l].astype(F32)
        sink_rows = sinks.reshape(N_KV_HEADS, GROUP).T.reshape(N_HEADS, 1)
        g1, g2, gf = row(norm1_g[l]), row(norm2_g[l]), row(final_norm_g)
        cb, lg, lb, bo = row(conv_b[l]), row(conv_ln_g[l]), row(conv_ln_b[l]), row(b_out[l])

        q, k, v, u, pk_l, pv_l, pc_l = _in_proj(hp, g1, w_in_l, b_in_l, *tabs_p, tm_in, batch)
        by_part = lambda a: a.reshape(a.shape[0], PARTS, KV_DIM).transpose(1, 0, 2)
        hp = _mlp_mix(hp, q, k, v, u, sinks, by_part(conv_w[l]), by_part(cb), lg, lb, w_out_l, bo, g2, gf,
                      w_up_l, w_down_l, seq, tm_mlp, 2, last)
        pk.append(pk_l.reshape(batch, WINDOW, N_KV_HEADS, HEAD_DIM))
        pv.append(pv_l.reshape(batch, WINDOW, N_KV_HEADS, HEAD_DIM))
        pc.append(pc_l)

        cache_kt = jnp.transpose(cache_k[l], (0, 2, 3, 1)).reshape(nsamp, KV_DIM, wb)
        cache_vt = jnp.transpose(cache_v[l], (0, 2, 3, 1)).reshape(nsamp, KV_DIM, wb)
        cache_ct = jnp.transpose(cache_conv[l], (1, 0, 2))
        bkv_c = b_in[l][o1:o2].reshape(2 * KV_DIM, 1)
        q, kt, vt, u = _in_proj_sample(hs, g1, w_in_l, b_in_l, bkv_c, tabs_s, tabs_sc)
        q_rep = jnp.repeat(q.reshape(nsamp, GROUP, KV_DIM), N_KV_HEADS, axis=1)
        by_block = lambda a: a.reshape(KV_DIM, nsamp // nseq, nseq).transpose(1, 0, 2)
        a8, nkt, nvt = _attn_sample(q_rep, by_block(kt), by_block(vt), cache_kt, cache_vt, sink_rows, nseq)
        c_o, nct = _conv_sample(cache_ct, u, conv_w[l], cb, lg, lb)
        a_o = a8[:, :GROUP, :].reshape(nsamp, Q_DIM).astype(BF16)
        hs = _out_mlp(hs, a_o, c_o, w_out_l, bo, g2, w_up_l, w_down_l, gf, nsamp, tf_s, last)
        to_cache = lambda a: jnp.transpose(a.reshape(nsamp, N_KV_HEADS, HEAD_DIM, wb), (0, 3, 1, 2))
        sk.append(to_cache(nkt))
        sv.append(to_cache(nvt))
        sc.append(jnp.transpose(nct, (1, 0, 2)))

    y_prompt = hp.reshape(batch, seq, d_model)
    y_sample = hs.reshape(nsamp, t_s, d_model)
    return (y_prompt, y_sample, jnp.stack(pk), jnp.stack(pv), jnp.stack(pc),
            jnp.stack(sk), jnp.stack(sv), jnp.stack(sc))
```

```python
import functools

import jax
import jax.numpy as jnp
import numpy as np
from jax import lax
from jax.experimental import pallas as pl
from jax.experimental.pallas import tpu as pltpu

HEAD_DIM = 64
N_HEADS = 16
N_KV_HEADS = 4
GROUP = N_HEADS // N_KV_HEADS
WINDOW = 128
ROT_DIM = HEAD_DIM // 4
ROPE_THETA = 500000.0
ATTN_SCALE = HEAD_DIM ** -0.5
Q_DIM = N_HEADS * HEAD_DIM
KV_DIM = N_KV_HEADS * HEAD_DIM
CONV_W = 31
PAST_LEN = 16384
EPS = 1e-5

LANES = 128
SUBLANES = 8
CONV_HALO = 32
CONV_ROWS = 64
VMEM_LIMIT_BYTES = 60 * 1024 * 1024
PARTS = GROUP
COL_CHUNK = 512

BF16 = jnp.bfloat16
F32 = jnp.float32


def _resident(shape):
    return pl.BlockSpec(shape, lambda *_: (0,) * len(shape), pipeline_mode=pl.Buffered(1))


def _params(*semantics):
    return pltpu.CompilerParams(dimension_semantics=semantics, vmem_limit_bytes=VMEM_LIMIT_BYTES)


def _rms_norm_f32(x, g):
    return x * lax.rsqrt(jnp.mean(x * x, axis=-1, keepdims=True) + EPS) * g


def _layer_norm_swish(y, g, b):
    mu = jnp.mean(y, axis=-1, keepdims=True)
    yc = y - mu
    var = jnp.mean(yc * yc, axis=-1, keepdims=True)
    yn = yc * lax.rsqrt(var + EPS) * g + b
    return yn * jax.nn.sigmoid(yn)


def _dot(a, b):
    return jnp.dot(a, b, preferred_element_type=F32)


def _dot_nt(a, b):
    return lax.dot_general(a, b, (((1,), (1,)), ((), ())), preferred_element_type=F32)


def _dot_chunks(a, w_ref, rows=slice(None)):
    return [_dot(a, w_ref[c, rows, :]) for c in range(w_ref.shape[0])]


def _sink_softmax(s, sink):
    m = jnp.maximum(jnp.max(s, axis=-1, keepdims=True), sink)
    p = jnp.exp(s - m)
    return p / (jnp.sum(p, axis=-1, keepdims=True) + jnp.exp(sink - m))


def _rope(z, cos, sin_up, sin_dn, axis):
    n = z.shape[axis]
    half = ROT_DIM // 2
    return z * cos + pltpu.roll(z, half, axis) * sin_up + pltpu.roll(z, n - half, axis) * sin_dn


def _rope_lanes(z, cos, sin_up, sin_dn):
    cols = [_rope(z[:, c * LANES:(c + 1) * LANES], cos, sin_up, sin_dn, 1) for c in range(z.shape[1] // LANES)]
    return jnp.concatenate(cols, axis=1)


def _in_proj_kernel(x_ref, g_ref, w_ref, b_ref, cos_ref, sup_ref, sdn_ref,
                    q_ref, k_ref, v_ref, u_ref, pk_ref, pv_ref, pc_ref):
    conv_ch = u_ref.shape[0] * u_ref.shape[2]
    tm = x_ref.shape[0]
    hist = pc_ref.shape[1]
    o1, o2, o3, o4 = Q_DIM, Q_DIM + KV_DIM, Q_DIM + 2 * KV_DIM, Q_DIM + 2 * KV_DIM + conv_ch
    hn = _rms_norm_f32(x_ref[...], g_ref[...]).astype(BF16)
    cos, sup, sdn = cos_ref[...], sup_ref[...], sdn_ref[...]

    zq = _dot(hn, w_ref[:, 0:o1]) + b_ref[:, 0:o1]
    q = (_rope_lanes(zq, cos, sup, sdn) * ATTN_SCALE).astype(q_ref.dtype)
    zk = _dot(hn, w_ref[:, o1:o2]) + b_ref[:, o1:o2]
    k_ref[...] = _rope_lanes(zk, cos, sup, sdn)
    v_ref[...] = _dot(hn, w_ref[:, o2:o3]) + b_ref[:, o2:o3]
    zu = _dot(hn, w_ref[:, o3:o4]) + b_ref[:, o3:o4]
    zg = _dot(hn, w_ref[:, o4:]) + b_ref[:, o4:]
    u = zu * jax.nn.sigmoid(zg)
    for p in range(PARTS):
        q_ref[p] = q[:, p * KV_DIM:(p + 1) * KV_DIM]
        u_ref[p] = u[:, p * KV_DIM:(p + 1) * KV_DIM]
    pk_ref[0] = k_ref[tm - WINDOW:, :]
    pv_ref[0] = v_ref[tm - WINDOW:, :]
    pc_ref[0] = u[tm - hist:, :]


def _in_proj(x, g, w, b, cos, sup, sdn, tm, batch):
    m, d = x.shape
    n = w.shape[1]
    conv_ch = (n - Q_DIM - 2 * KV_DIM) // 2
    nt = cos.shape[0] // tm
    hist = CONV_W - 1
    row = lambda i: (i, 0)
    parts = lambda i: (0, i, 0)
    tab = lambda i: (i % nt, 0)
    seq = lambda i: (i // nt, 0, 0)
    assert conv_ch == PARTS * KV_DIM and Q_DIM == PARTS * KV_DIM
    return pl.pallas_call(
        _in_proj_kernel,
        grid=(m // tm,),
        in_specs=[pl.BlockSpec((tm, d), row), _resident((1, d)), _resident((d, n)), _resident((1, n)),
                  pl.BlockSpec((tm, LANES), tab), pl.BlockSpec((tm, LANES), tab),
                  pl.BlockSpec((tm, LANES), tab)],
        out_specs=[pl.BlockSpec((PARTS, tm, KV_DIM), parts), pl.BlockSpec((tm, KV_DIM), row),
                   pl.BlockSpec((tm, KV_DIM), row), pl.BlockSpec((PARTS, tm, KV_DIM), parts),
                   pl.BlockSpec((1, WINDOW, KV_DIM), seq), pl.BlockSpec((1, WINDOW, KV_DIM), seq),
                   pl.BlockSpec((1, hist, conv_ch), seq)],
        out_shape=[jax.ShapeDtypeStruct((PARTS, m, KV_DIM), BF16), jax.ShapeDtypeStruct((m, KV_DIM), F32),
                   jax.ShapeDtypeStruct((m, KV_DIM), F32), jax.ShapeDtypeStruct((PARTS, m, KV_DIM), F32),
                   jax.ShapeDtypeStruct((batch, WINDOW, KV_DIM), F32),
                   jax.ShapeDtypeStruct((batch, WINDOW, KV_DIM), F32),
                   jax.ShapeDtypeStruct((batch, hist, conv_ch), F32)],
        compiler_params=_params("arbitrary"),
        name="in_proj",
    )(x, g, w, b, cos, sup, sdn)


def _attn_scores(q, kk, vv, has_prev):
    blk = q.shape[0]
    kk, vv = kk.astype(BF16), vv.astype(BF16)
    lane_group = lax.broadcasted_iota(jnp.int32, kk.shape, 1) // HEAD_DIM
    zero = jnp.zeros_like(kk)
    kstack = jnp.concatenate([jnp.where(lane_group == g, kk, zero) for g in range(N_KV_HEADS)], axis=0)
    vstack = jnp.concatenate([jnp.where(lane_group == g, vv, zero) for g in range(N_KV_HEADS)], axis=0)
    qi = lax.broadcasted_iota(jnp.int32, (blk, 2 * blk), 0)
    kj = lax.broadcasted_iota(jnp.int32, (blk, 2 * blk), 1)
    rel = qi + blk - kj
    valid = (rel >= 0) & (rel < WINDOW) & (has_prev | (kj >= blk))
    return _dot_nt(q, kstack), valid, vstack


def _attn_probs(s_all, valid, sink_ref, part, zero_tile):
    width = valid.shape[1]
    probs = []
    for g in range(N_KV_HEADS):
        s = jnp.where(valid, s_all[:, g * width:(g + 1) * width], -jnp.inf)
        p = _sink_softmax(s, sink_ref[g * GROUP + part])
        if g == 0:
            top = jnp.concatenate([p[0:SUBLANES, 0:LANES] + zero_tile, p[0:SUBLANES, LANES:]], axis=1)
            p = jnp.concatenate([top, p[SUBLANES:, :]], axis=0)
        probs.append(p.astype(BF16))
    return jnp.concatenate(probs, axis=1)


def _conv_lanes(c, cw_ref, cb_ref, ue_ref, row0, blk, c_out):
    first = CONV_HALO - (CONV_W - 1)
    cs = slice(c * LANES, (c + 1) * LANES)
    seen = None
    for s0 in range(0, blk, CONV_ROWS):
        win = ue_ref[pl.ds(pl.multiple_of(row0 + s0, SUBLANES), CONV_ROWS + CONV_HALO), cs]
        out = cb_ref[:, cs]
        for r in range(SUBLANES):
            rows = CONV_ROWS + (SUBLANES if r else 0)
            part = None
            for a in range((first + CONV_W - 1 - r) // SUBLANES + 1):
                t = a * SUBLANES + r - first
                if 0 <= t < CONV_W:
                    term = cw_ref[t:t + 1, cs] * win[a * SUBLANES:a * SUBLANES + rows, :]
                    part = term if part is None else part + term
            out = out + part[r:r + CONV_ROWS, :]
        c_out[s0:s0 + CONV_ROWS, cs] = out
        for r0 in range(0, CONV_ROWS, SUBLANES):
            tile = out[r0:r0 + SUBLANES, :]
            seen = tile if seen is None else jnp.maximum(seen, tile)
    return seen


def _mlp_mix_kernel(x_ref, q_ref, k_ref, kp_ref, v_ref, vp_ref, u_ref, uh_ref, sink_ref, cw_ref, cb_ref,
                    lg_ref, lb_ref, wo_ref, bo_ref, g2_ref, gf_ref, wu_ref, wd_ref,
                    o_ref, hn_ref, a_s, c_s, kbuf, vbuf, ubuf, *, nb, nrow, parts_per_step, final_norm):
    i, f = pl.program_id(0), pl.program_id(1)
    blk = WINDOW
    blocks_per_tile = x_ref.shape[0] // blk
    steps_per_block = PARTS // parts_per_step

    @pl.when((i == 0) & (f == 0))
    def _():
        a_s[...] = jnp.zeros_like(a_s)
        c_s[...] = jnp.zeros_like(c_s)

    @pl.when(f == 0)
    def _():
        a = jnp.concatenate([a_s[p] for p in range(PARTS)], axis=1)
        conv = jnp.concatenate([c_s[p] for p in range(PARTS)], axis=1)
        c = _layer_norm_swish(conv, lg_ref[...], lb_ref[...]).astype(BF16)
        mix = [ma + mc for ma, mc in zip(_dot_chunks(a, wo_ref, slice(0, Q_DIM)),
                                         _dot_chunks(c, wo_ref, slice(Q_DIM, None)))]
        h = x_ref[...] + jnp.concatenate(mix, axis=1) + bo_ref[...]
        o_ref[...] = h
        hn_ref[...] = _rms_norm_f32(h, g2_ref[...]).astype(hn_ref.dtype)
        kbuf[0:blk, :] = kp_ref[...]
        kbuf[blk:, :] = k_ref[...]
        vbuf[0:blk, :] = vp_ref[...]
        vbuf[blk:, :] = v_ref[...]
        continues = (jnp.minimum(i, nrow - 1) * blocks_per_tile) % nb != 0
        for p in range(PARTS):
            ubuf[p, 0:CONV_HALO, :] = jnp.where(continues, uh_ref[p], jnp.zeros_like(uh_ref[p]))
            ubuf[p, CONV_HALO:, :] = u_ref[p]

    sub, part0 = f // steps_per_block, (f % steps_per_block) * parts_per_step
    parts = [part0 + j for j in range(parts_per_step)]
    gblk = jnp.minimum(i, nrow - 1) * blocks_per_tile + sub
    has_prev = gblk % nb != 0
    rows = pl.ds(pl.multiple_of(sub * blk, blk), blk)
    both = pl.ds(pl.multiple_of(sub * blk, blk), 2 * blk)
    conv_done = []
    for part in parts:
        conv_out = c_s.at[part, rows, :]
        seen = jnp.maximum(_conv_lanes(0, cw_ref.at[part], cb_ref.at[part], ubuf.at[part], sub * blk, blk, conv_out),
                           _conv_lanes(1, cw_ref.at[part], cb_ref.at[part], ubuf.at[part], sub * blk, blk, conv_out))
        conv_done.append(jnp.minimum(jnp.abs(seen), 0.0))
    scores = [_attn_scores(q_ref[part, rows, :], kbuf[both, :], vbuf[both, :], has_prev) for part in parts]
    up = jnp.concatenate(_dot_chunks(hn_ref[...], wu_ref), axis=1)
    act = jnp.square(jnp.maximum(up, 0.0)).astype(BF16)
    for part, (s_all, valid, vstack), zero in zip(parts, scores, conv_done):
        probs = _attn_probs(s_all, valid, sink_ref, part, zero)
        a_s[part, rows, :] = _dot(probs, vstack).astype(a_s.dtype)
    for c, down in enumerate(_dot_chunks(act, wd_ref)):
        o_ref[:, c * COL_CHUNK:(c + 1) * COL_CHUNK] += down

    if final_norm:
        @pl.when(f == pl.num_programs(1) - 1)
        def _():
            o_ref[...] = _rms_norm_f32(o_ref[...], gf_ref[...])


def _mlp_mix(x, q, k, v, u, sinks, conv_w, conv_b, ln_g, ln_b, w_out, b_out, g2, gf, w_up, w_down,
             seq, tm, parts_per_step, final_norm):
    m, d = x.shape
    conv_ch = u.shape[0] * u.shape[2]
    ff = w_down.shape[1]
    blk = WINDOW
    nb = seq // blk
    nrow = m // tm
    bpt = tm // blk
    nf = bpt * PARTS // parts_per_step
    assert ff % (nf * COL_CHUNK) == 0 and conv_ch == PARTS * KV_DIM and Q_DIM == PARTS * KV_DIM
    assert seq % tm == 0
    tf = ff // nf
    hpb = blk // CONV_HALO
    mix_row = lambda i: jnp.minimum(i, nrow - 1)
    mlp_tile = lambda i, f: (jnp.maximum(i - 1, 0), 0)
    mix_tile = lambda i, f: (mix_row(i), 0)
    mix_parts = lambda i, f: (0, mix_row(i), 0)
    prev = lambda i, f: (jnp.maximum(mix_row(i) * bpt - 1, 0), 0)
    halo_parts = lambda i, f: (0, jnp.maximum(mix_row(i) * bpt * hpb - 1, 0), 0)
    return pl.pallas_call(
        functools.partial(_mlp_mix_kernel, nb=nb, nrow=nrow, parts_per_step=parts_per_step,
                          final_norm=final_norm),
        grid=(nrow + 1, nf),
        in_specs=[pl.BlockSpec((tm, d), mlp_tile),
                  pl.BlockSpec((PARTS, tm, KV_DIM), mix_parts),
                  pl.BlockSpec((tm, KV_DIM), mix_tile), pl.BlockSpec((blk, KV_DIM), prev),
                  pl.BlockSpec((tm, KV_DIM), mix_tile), pl.BlockSpec((blk, KV_DIM), prev),
                  pl.BlockSpec((PARTS, tm, KV_DIM), mix_parts),
                  pl.BlockSpec((PARTS, CONV_HALO, KV_DIM), halo_parts),
                  pl.BlockSpec(memory_space=pltpu.SMEM),
                  _resident((PARTS, CONV_W, KV_DIM)), _resident((PARTS, 1, KV_DIM)),
                  _resident((1, conv_ch)), _resident((1, conv_ch)),
                  _resident(w_out.shape), _resident((1, d)), _resident((1, d)), _resident((1, d)),
                  pl.BlockSpec((tf // COL_CHUNK, d, COL_CHUNK), lambda i, f: (f, 0, 0)),
                  pl.BlockSpec((w_down.shape[0], tf, COL_CHUNK), lambda i, f: (0, f, 0))],
        out_specs=pl.BlockSpec((tm, d), mlp_tile),
        out_shape=jax.ShapeDtypeStruct((m, d), F32),
        scratch_shapes=[pltpu.VMEM((tm, d), BF16),
                        pltpu.VMEM((PARTS, tm, KV_DIM), BF16),
                        pltpu.VMEM((PARTS, tm, KV_DIM), F32),
                        pltpu.VMEM((blk + tm, KV_DIM), F32), pltpu.VMEM((blk + tm, KV_DIM), F32),
                        pltpu.VMEM((PARTS, CONV_HALO + tm, KV_DIM), F32)],
        compiler_params=_params("arbitrary", "arbitrary"),
        name="mlp_mix",
    )(x, q, k, k, v, v, u, u, sinks, conv_w, conv_b, ln_g, ln_b, w_out, b_out, g2, gf, w_up, w_down)


def _in_proj_sample_kernel(x_ref, g_ref, w_ref, b_ref, bkv_ref,
                           cos_ref, sup_ref, sdn_ref, cosc_ref, supc_ref, sdnc_ref,
                           q_ref, kt_ref, vt_ref, u_ref):
    conv_ch = u_ref.shape[1]
    o2, o3 = Q_DIM + 2 * KV_DIM, Q_DIM + 2 * KV_DIM + conv_ch
    hn = _rms_norm_f32(x_ref[...], g_ref[...]).astype(BF16)
    zq = _dot(hn, w_ref[:, 0:Q_DIM]) + b_ref[:, 0:Q_DIM]
    q_ref[...] = _rope_lanes(zq, cos_ref[...], sup_ref[...], sdn_ref[...]) * ATTN_SCALE
    zkv = lax.dot_general(w_ref[:, Q_DIM:o2], hn, (((0,), (1,)), ((), ())), preferred_element_type=F32) + bkv_ref[...]
    kt_ref[...] = _rope(zkv[0:KV_DIM, :], cosc_ref[...], supc_ref[...], sdnc_ref[...], 0)
    vt_ref[...] = zkv[KV_DIM:, :]
    zu = _dot(hn, w_ref[:, o2:o3]) + b_ref[:, o2:o3]
    zg = _dot(hn, w_ref[:, o3:]) + b_ref[:, o3:]
    u_ref[...] = zu * jax.nn.sigmoid(zg)


def _in_proj_sample(x, g, w, b, bkv_c, row_tabs, col_tabs):
    n, d = x.shape
    conv_ch = (w.shape[1] - Q_DIM - 2 * KV_DIM) // 2
    full = lambda a: _resident(a.shape)
    args = (x, g, w, b, bkv_c, *row_tabs, *col_tabs)
    return pl.pallas_call(
        _in_proj_sample_kernel,
        grid=(1,),
        in_specs=[full(a) for a in args],
        out_specs=[_resident((n, Q_DIM)), _resident((KV_DIM, n)), _resident((KV_DIM, n)), _resident((n, conv_ch))],
        out_shape=[jax.ShapeDtypeStruct((n, Q_DIM), F32), jax.ShapeDtypeStruct((KV_DIM, n), F32),
                   jax.ShapeDtypeStruct((KV_DIM, n), F32), jax.ShapeDtypeStruct((n, conv_ch), F32)],
        compiler_params=_params("arbitrary"),
        name="in_proj_sample",
    )(*args)


def _attn_sample_kernel(q_ref, kn_ref, vn_ref, ck_ref, cv_ref, sink_ref, ao_ref, nk_ref, nv_ref):
    nseq, _, wb = ck_ref.shape
    rows = N_HEADS
    lane_group = lax.broadcasted_iota(jnp.int32, (rows, KV_DIM), 1) // HEAD_DIM
    row_group = lax.broadcasted_iota(jnp.int32, (rows, KV_DIM), 0) % N_KV_HEADS
    own_lanes = lane_group == row_group
    fold = (lax.broadcasted_iota(jnp.int32, (SUBLANES, rows), 1) // N_KV_HEADS
            == lax.broadcasted_iota(jnp.int32, (SUBLANES, rows), 0)).astype(F32)
    newest = lax.broadcasted_iota(jnp.int32, (KV_DIM, wb), 1) == wb - 1
    sink = sink_ref[...]
    kn, vn = kn_ref[0], vn_ref[0]

    seqs = range(nseq)
    knew = [jnp.where(newest, kn[:, n:n + 1], pltpu.roll(ck_ref[n], wb - 1, 1)) for n in seqs]
    vnew = [jnp.where(newest, vn[:, n:n + 1], pltpu.roll(cv_ref[n], wb - 1, 1)) for n in seqs]
    for n in seqs:
        nk_ref[n] = knew[n]
        nv_ref[n] = vnew[n]
    qrows = [jnp.where(own_lanes, q_ref[n], jnp.zeros((rows, KV_DIM), F32)).astype(BF16) for n in seqs]
    s = [_dot(qrows[n], knew[n].astype(BF16)) for n in seqs]
    p = [_sink_softmax(s[n], sink).astype(BF16) for n in seqs]
    o = [_dot_nt(p[n], vnew[n].astype(BF16)) for n in seqs]
    for n in seqs:
        ao_ref[n] = _dot(fold, jnp.where(own_lanes, o[n], jnp.zeros_like(o[n])))


def _attn_sample(q_rep, kn, vn, cache_kt, cache_vt, sink_rows, nseq):
    n, _, wb = cache_kt.shape
    r3 = lambda s: (s, 0, 0)
    return pl.pallas_call(
        _attn_sample_kernel,
        grid=(n // nseq,),
        in_specs=[pl.BlockSpec((nseq, N_HEADS, KV_DIM), r3),
                  pl.BlockSpec((1, KV_DIM, nseq), r3), pl.BlockSpec((1, KV_DIM, nseq), r3),
                  pl.BlockSpec((nseq, KV_DIM, wb), r3), pl.BlockSpec((nseq, KV_DIM, wb), r3),
                  _resident((N_HEADS, 1))],
        out_specs=[pl.BlockSpec((nseq, SUBLANES, KV_DIM), r3),
                   pl.BlockSpec((nseq, KV_DIM, wb), r3), pl.BlockSpec((nseq, KV_DIM, wb), r3)],
        out_shape=[jax.ShapeDtypeStruct((n, SUBLANES, KV_DIM), F32),
                   jax.ShapeDtypeStruct((n, KV_DIM, wb), F32), jax.ShapeDtypeStruct((n, KV_DIM, wb), F32)],
        compiler_params=_params("arbitrary"),
        name="attn_sample",
    )(q_rep, kn, vn, cache_kt, cache_vt, sink_rows)


def _conv_sample_kernel(cc_ref, cn_ref, u_ref, cw_ref, cb_ref, lg_ref, lb_ref, co_ref, nc_ref, acc_ref, *, hist):
    j = pl.program_id(0)
    taps = cc_ref.shape[0]
    steps = hist // taps
    u = u_ref[...]

    @pl.when(j == 0)
    def _():
        acc_ref[...] = jnp.zeros_like(acc_ref)

    acc = acc_ref[...]
    for t in range(taps):
        acc = acc + cw_ref[j * taps + t] * cc_ref[t]
    acc_ref[...] = acc
    for t in range(taps - 1):
        nc_ref[t] = cc_ref[t + 1]

    @pl.when(j < steps - 1)
    def _():
        nc_ref[taps - 1] = cn_ref[0]

    @pl.when(j == steps - 1)
    def _():
        nc_ref[taps - 1] = u
        y = acc + cw_ref[hist] * u + cb_ref[...]
        co_ref[...] = _layer_norm_swish(y, lg_ref[...], lb_ref[...]).astype(co_ref.dtype)


def _conv_sample(cache_t, u, conv_w, conv_b, ln_g, ln_b, taps):
    hist, n, ch = cache_t.shape
    assert hist % taps == 0
    blk = lambda j: (j, 0, 0)
    return pl.pallas_call(
        functools.partial(_conv_sample_kernel, hist=hist),
        grid=(hist // taps,),
        in_specs=[pl.BlockSpec((taps, n, ch), blk),
                  pl.BlockSpec((1, n, ch), lambda j: (jnp.minimum((j + 1) * taps, hist - 1), 0, 0)),
                  _resident((n, ch)), _resident((hist + 1, 1, ch)),
                  _resident((1, ch)), _resident((1, ch)), _resident((1, ch))],
        out_specs=[_resident((n, ch)), pl.BlockSpec((taps, n, ch), blk)],
        out_shape=[jax.ShapeDtypeStruct((n, ch), BF16), jax.ShapeDtypeStruct((hist, n, ch), F32)],
        scratch_shapes=[pltpu.VMEM((n, ch), F32)],
        compiler_params=_params("arbitrary"),
        name="conv_sample",
    )(cache_t, cache_t, u, conv_w.reshape(hist + 1, 1, ch), conv_b, ln_g, ln_b)


def _out_mlp_kernel(x_ref, a_ref, c_ref, wo_ref, bo_ref, g2_ref, wu_ref, wd_ref, gf_ref,
                    o_ref, hn_ref, *, final_norm):
    f = pl.program_id(1)

    @pl.when(f == 0)
    def _():
        mix = [ma + mc for ma, mc in zip(_dot_chunks(a_ref[...], wo_ref, slice(0, Q_DIM)),
                                         _dot_chunks(c_ref[...], wo_ref, slice(Q_DIM, None)))]
        h = x_ref[...] + jnp.concatenate(mix, axis=1) + bo_ref[...]
        o_ref[...] = h
        hn_ref[...] = _rms_norm_f32(h, g2_ref[...]).astype(hn_ref.dtype)

    up = jnp.concatenate(_dot_chunks(hn_ref[...], wu_ref), axis=1)
    act = jnp.square(jnp.maximum(up, 0.0)).astype(BF16)
    for c, down in enumerate(_dot_chunks(act, wd_ref)):
        o_ref[:, c * COL_CHUNK:(c + 1) * COL_CHUNK] += down

    if final_norm:
        @pl.when(f == pl.num_programs(1) - 1)
        def _():
            o_ref[...] = _rms_norm_f32(o_ref[...], gf_ref[...])


def _out_mlp(x, a, c, w_out, b_out, g2, w_up, w_down, gf, tm, tf, final_norm):
    m, d = x.shape
    conv_ch = w_out.shape[1] - Q_DIM
    ff = w_down.shape[1]
    row = lambda i, f: (i, 0)
    return pl.pallas_call(
        functools.partial(_out_mlp_kernel, final_norm=final_norm),
        grid=(m // tm, ff // tf),
        in_specs=[pl.BlockSpec((tm, d), row), pl.BlockSpec((tm, Q_DIM), row),
                  pl.BlockSpec((tm, conv_ch), row),
                  _resident(w_out.shape), _resident((1, d)), _resident((1, d)),
                  pl.BlockSpec((tf // COL_CHUNK, d, COL_CHUNK), lambda i, f: (f, 0, 0)),
                  pl.BlockSpec((w_down.shape[0], tf, COL_CHUNK), lambda i, f: (0, f, 0)),
                  _resident((1, d))],
        out_specs=pl.BlockSpec((tm, d), row),
        out_shape=jax.ShapeDtypeStruct((m, d), F32),
        scratch_shapes=[pltpu.VMEM((tm, d), BF16)],
        compiler_params=_params("arbitrary", "arbitrary"),
        name="out_mlp",
    )(x, a, c, w_out, b_out, g2, w_up, w_down, gf)


def _rope_tables(pos):
    half = ROT_DIM // 2
    inv_freq = jnp.power(jnp.float32(ROPE_THETA), -jnp.arange(half, dtype=jnp.float32) * 2.0 / ROT_DIM)
    ang = pos.astype(jnp.float32)[:, None] * inv_freq[None, :]
    cos, sin = jnp.cos(ang), jnp.sin(ang)
    t = pos.shape[0]
    pad = jnp.zeros((t, HEAD_DIM - ROT_DIM), F32)
    zeros = jnp.zeros((t, half), F32)
    per_head = lambda a, b, fill: jnp.concatenate([a, b, pad + fill], axis=1)
    reps = LANES // HEAD_DIM
    cos_t = jnp.tile(per_head(cos, cos, 1.0), (1, reps))
    sup_t = jnp.tile(per_head(zeros, sin, 0.0), (1, reps))
    sdn_t = jnp.tile(per_head(-sin, zeros, 0.0), (1, reps))
    return cos_t, sup_t, sdn_t


def _heads_to_igd(a, axis):
    shape = a.shape
    a = a.reshape(shape[:axis] + (N_KV_HEADS, GROUP, HEAD_DIM) + shape[axis + 1:])
    return jnp.swapaxes(a, axis, axis + 1).reshape(shape)


def _cast_slabs_kernel(w_ref, o_ref):
    o_ref[0] = w_ref[...].astype(o_ref.dtype)


def _cast_slabs(w, tk):
    k, n = w.shape
    return pl.pallas_call(
        _cast_slabs_kernel,
        grid=(n // COL_CHUNK, k // tk),
        in_specs=[pl.BlockSpec((tk, COL_CHUNK), lambda j, kk: (kk, j))],
        out_specs=pl.BlockSpec((1, tk, COL_CHUNK), lambda j, kk: (j, kk, 0)),
        out_shape=jax.ShapeDtypeStruct((n // COL_CHUNK, k, COL_CHUNK), BF16),
        compiler_params=_params("arbitrary", "arbitrary"),
        name="cast_slabs",
    )(w)


def _igd_source(idx):
    i, g, d = idx // KV_DIM, (idx // HEAD_DIM) % N_KV_HEADS, idx % HEAD_DIM
    return (g * GROUP + i) * HEAD_DIM + d


def _prep_w_in_kernel(w_ref, o_ref):
    w = w_ref[...].astype(o_ref.dtype)
    src = lax.broadcasted_iota(jnp.int32, (Q_DIM, Q_DIM), 0)
    dst = lax.broadcasted_iota(jnp.int32, (Q_DIM, Q_DIM), 1)
    move = jnp.where(src == _igd_source(dst), 1.0, 0.0).astype(o_ref.dtype)
    o_ref[:, 0:Q_DIM] = _dot(w[:, 0:Q_DIM], move).astype(o_ref.dtype)
    o_ref[:, Q_DIM:] = w[:, Q_DIM:]


def _prep_w_in(w, tk):
    k, n = w.shape
    return pl.pallas_call(
        _prep_w_in_kernel,
        grid=(k // tk,),
        in_specs=[pl.BlockSpec((tk, n), lambda j: (j, 0))],
        out_specs=pl.BlockSpec((tk, n), lambda j: (j, 0)),
        out_shape=jax.ShapeDtypeStruct((k, n), BF16),
        compiler_params=_params("arbitrary"),
        name="prep_w_in",
    )(w)


def _prep_w_out_kernel(w_ref, o_ref):
    w = w_ref[...].astype(o_ref.dtype)
    dst = lax.broadcasted_iota(jnp.int32, (Q_DIM, Q_DIM), 0)
    src = lax.broadcasted_iota(jnp.int32, (Q_DIM, Q_DIM), 1)
    move = jnp.where(src == _igd_source(dst), 1.0, 0.0).astype(o_ref.dtype)
    o_ref[0, 0:Q_DIM, :] = _dot(move, w[0:Q_DIM, :]).astype(o_ref.dtype)
    o_ref[0, Q_DIM:, :] = w[Q_DIM:, :]


def _prep_w_out(w):
    k, n = w.shape
    return pl.pallas_call(
        _prep_w_out_kernel,
        grid=(n // COL_CHUNK,),
        in_specs=[pl.BlockSpec((k, COL_CHUNK), lambda c: (0, c))],
        out_specs=pl.BlockSpec((1, k, COL_CHUNK), lambda c: (c, 0, 0)),
        out_shape=jax.ShapeDtypeStruct((n // COL_CHUNK, k, COL_CHUNK), BF16),
        compiler_params=_params("arbitrary"),
        name="prep_w_out",
    )(w)


def _tile(m, target):
    t = min(m, target)
    assert m % t == 0, (m, t)
    return t


def kernel(x_prompt, x_sample, cache_k, cache_v, cache_conv, norm1_g, w_in, b_in, attn_sinks,
           conv_w, conv_b, conv_ln_g, conv_ln_b, w_out, b_out, norm2_g, w_up, w_down, final_norm_g):
    batch, seq, d_model = x_prompt.shape
    nsamp, t_s, _ = x_sample.shape
    depth = w_in.shape[0]
    conv_ch = conv_w.shape[2]
    wb = cache_k.shape[2]
    hist = CONV_W - 1
    assert t_s == 1 and seq % WINDOW == 0 and wb == WINDOW and cache_conv.shape[2] == hist
    assert depth >= 1

    tabs_p = _rope_tables(jnp.arange(seq, dtype=jnp.int32))
    tabs_s1 = _rope_tables(PAST_LEN + jnp.arange(t_s, dtype=jnp.int32))
    tabs_s = tuple(jnp.broadcast_to(t, (nsamp, LANES)) for t in tabs_s1)
    tabs_sc = tuple(jnp.tile(t[0], KV_DIM // LANES).reshape(KV_DIM, 1) for t in tabs_s1)
    row = lambda a: a.reshape(1, -1)

    hp = x_prompt.reshape(batch * seq, d_model)
    hs = x_sample.reshape(nsamp * t_s, d_model)
    tm_in = _tile(seq, 512)
    tm_mlp = _tile(batch * seq, 512)
    tf_s = 2 * COL_CHUNK
    nseq = _tile(nsamp, 8)
    o1, o2 = Q_DIM, Q_DIM + 2 * KV_DIM
    pk, pv, pc, sk, sv, sc = [], [], [], [], [], []
    for l in range(depth):
        last = l == depth - 1
        w_in_l = _prep_w_in(w_in[l], _tile(d_model, 512))
        b_in_l = row(jnp.concatenate([_heads_to_igd(b_in[l][:o1], 0), b_in[l][o1:]]))
        w_out_l = _prep_w_out(w_out[l])
        w_up_l, w_down_l = _cast_slabs(w_up[l], _tile(d_model, 2048)), _cast_slabs(w_down[l], 2048)
        sinks = attn_sinks[l].astype(F32)
        sink_rows = sinks.reshape(N_KV_HEADS, GROUP).T.reshape(N_HEADS, 1)
        g1, g2, gf = row(norm1_g[l]), row(norm2_g[l]), row(final_norm_g)
        cb, lg, lb, bo = row(conv_b[l]), row(conv_ln_g[l]), row(conv_ln_b[l]), row(b_out[l])

        q, k, v, u, pk_l, pv_l, pc_l = _in_proj(hp, g1, w_in_l, b_in_l, *tabs_p, tm_in, batch)
        by_part = lambda a: a.reshape(a.shape[0], PARTS, KV_DIM).transpose(1, 0, 2)
        hp = _mlp_mix(hp, q, k, v, u, sinks, by_part(conv_w[l]), by_part(cb), lg, lb, w_out_l, bo, g2, gf,
                      w_up_l, w_down_l, seq, tm_mlp, 1, last)
        pk.append(pk_l.reshape(batch, WINDOW, N_KV_HEADS, HEAD_DIM))
        pv.append(pv_l.reshape(batch, WINDOW, N_KV_HEADS, HEAD_DIM))
        pc.append(pc_l)

        cache_kt = jnp.transpose(cache_k[l], (0, 2, 3, 1)).reshape(nsamp, KV_DIM, wb)
        cache_vt = jnp.transpose(cache_v[l], (0, 2, 3, 1)).reshape(nsamp, KV_DIM, wb)
        cache_ct = jnp.transpose(cache_conv[l], (1, 0, 2))
        bkv_c = b_in[l][o1:o2].reshape(2 * KV_DIM, 1)
        q, kt, vt, u = _in_proj_sample(hs, g1, w_in_l, b_in_l, bkv_c, tabs_s, tabs_sc)
        q_rep = jnp.repeat(q.reshape(nsamp, GROUP, KV_DIM), N_KV_HEADS, axis=1)
        by_block = lambda a: a.reshape(KV_DIM, nsamp // nseq, nseq).transpose(1, 0, 2)
        a8, nkt, nvt = _attn_sample(q_rep, by_block(kt), by_block(vt), cache_kt, cache_vt, sink_rows, nseq)
        c_o, nct = _conv_sample(cache_ct, u, conv_w[l], cb, lg, lb, 5)
        a_o = a8[:, :GROUP, :].reshape(nsamp, Q_DIM).astype(BF16)
        hs = _out_mlp(hs, a_o, c_o, w_out_l, bo, g2, w_up_l, w_down_l, gf, nsamp, tf_s, last)
        to_cache = lambda a: jnp.transpose(a.reshape(nsamp, N_KV_HEADS, HEAD_DIM, wb), (0, 3, 1, 2))
        sk.append(to_cache(nkt))
        sv.append(to_cache(nvt))
        sc.append(jnp.transpose(nct, (1, 0, 2)))

    y_prompt = hp.reshape(batch, seq, d_model)
    y_sample = hs.reshape(nsamp, t_s, d_model)
    return (y_prompt, y_sample, jnp.stack(pk), jnp.stack(pv), jnp.stack(pc),
            jnp.stack(sk), jnp.stack(sv), jnp.stack(sc))
```

```python
import functools

import jax
import jax.numpy as jnp
import numpy as np
from jax import lax
from jax.experimental import pallas as pl
from jax.experimental.pallas import tpu as pltpu

HEAD_DIM = 64
N_HEADS = 16
N_KV_HEADS = 4
GROUP = N_HEADS // N_KV_HEADS
WINDOW = 128
ROT_DIM = HEAD_DIM // 4
ROPE_THETA = 500000.0
ATTN_SCALE = HEAD_DIM ** -0.5
Q_DIM = N_HEADS * HEAD_DIM
KV_DIM = N_KV_HEADS * HEAD_DIM
CONV_W = 31
PAST_LEN = 16384
EPS = 1e-5

LANES = 128
SUBLANES = 8
CONV_HALO = 32
CONV_ROWS = 64
VMEM_LIMIT_BYTES = 60 * 1024 * 1024
PARTS = GROUP
COL_CHUNK = 512

BF16 = jnp.bfloat16
F32 = jnp.float32


def _resident(shape):
    return pl.BlockSpec(shape, lambda *_: (0,) * len(shape), pipeline_mode=pl.Buffered(1))


def _params(*semantics):
    return pltpu.CompilerParams(dimension_semantics=semantics, vmem_limit_bytes=VMEM_LIMIT_BYTES)


def _rms_norm_f32(x, g):
    return x * lax.rsqrt(jnp.mean(x * x, axis=-1, keepdims=True) + EPS) * g


def _layer_norm_swish(y, g, b):
    mu = jnp.mean(y, axis=-1, keepdims=True)
    yc = y - mu
    var = jnp.mean(yc * yc, axis=-1, keepdims=True)
    yn = yc * lax.rsqrt(var + EPS) * g + b
    return yn * jax.nn.sigmoid(yn)


def _dot(a, b):
    return jnp.dot(a, b, preferred_element_type=F32)


def _dot_nt(a, b):
    return lax.dot_general(a, b, (((1,), (1,)), ((), ())), preferred_element_type=F32)


def _dot_chunks(a, w_ref, rows=slice(None)):
    return [_dot(a, w_ref[c, rows, :]) for c in range(w_ref.shape[0])]


def _sink_softmax(s, sink):
    m = jnp.maximum(jnp.max(s, axis=-1, keepdims=True), sink)
    p = jnp.exp(s - m)
    return p / (jnp.sum(p, axis=-1, keepdims=True) + jnp.exp(sink - m))


def _rope(z, cos, sin_up, sin_dn, axis):
    n = z.shape[axis]
    half = ROT_DIM // 2
    return z * cos + pltpu.roll(z, half, axis) * sin_up + pltpu.roll(z, n - half, axis) * sin_dn


def _rope_lanes(z, cos, sin_up, sin_dn):
    cols = [_rope(z[:, c * LANES:(c + 1) * LANES], cos, sin_up, sin_dn, 1) for c in range(z.shape[1] // LANES)]
    return jnp.concatenate(cols, axis=1)


def _in_proj_kernel(x_ref, g_ref, w_ref, b_ref, cos_ref, sup_ref, sdn_ref,
                    q_ref, k_ref, v_ref, u_ref, pk_ref, pv_ref, pc_ref):
    conv_ch = u_ref.shape[0] * u_ref.shape[2]
    tm = x_ref.shape[0]
    hist = pc_ref.shape[1]
    o1, o2, o3, o4 = Q_DIM, Q_DIM + KV_DIM, Q_DIM + 2 * KV_DIM, Q_DIM + 2 * KV_DIM + conv_ch
    hn = _rms_norm_f32(x_ref[...], g_ref[...]).astype(BF16)
    cos, sup, sdn = cos_ref[...], sup_ref[...], sdn_ref[...]

    zq = _dot(hn, w_ref[:, 0:o1]) + b_ref[:, 0:o1]
    q = (_rope_lanes(zq, cos, sup, sdn) * ATTN_SCALE).astype(q_ref.dtype)
    zk = _dot(hn, w_ref[:, o1:o2]) + b_ref[:, o1:o2]
    k_ref[...] = _rope_lanes(zk, cos, sup, sdn)
    v_ref[...] = _dot(hn, w_ref[:, o2:o3]) + b_ref[:, o2:o3]
    zu = _dot(hn, w_ref[:, o3:o4]) + b_ref[:, o3:o4]
    zg = _dot(hn, w_ref[:, o4:]) + b_ref[:, o4:]
    u = zu * jax.nn.sigmoid(zg)
    for p in range(PARTS):
        q_ref[p] = q[:, p * KV_DIM:(p + 1) * KV_DIM]
        u_ref[p] = u[:, p * KV_DIM:(p + 1) * KV_DIM]
    pk_ref[0] = k_ref[tm - WINDOW:, :]
    pv_ref[0] = v_ref[tm - WINDOW:, :]
    pc_ref[0] = u[tm - hist:, :]


def _in_proj(x, g, w, b, cos, sup, sdn, tm, batch):
    m, d = x.shape
    n = w.shape[1]
    conv_ch = (n - Q_DIM - 2 * KV_DIM) // 2
    nt = cos.shape[0] // tm
    hist = CONV_W - 1
    row = lambda i: (i, 0)
    parts = lambda i: (0, i, 0)
    tab = lambda i: (i % nt, 0)
    seq = lambda i: (i // nt, 0, 0)
    assert conv_ch == PARTS * KV_DIM and Q_DIM == PARTS * KV_DIM
    return pl.pallas_call(
        _in_proj_kernel,
        grid=(m // tm,),
        in_specs=[pl.BlockSpec((tm, d), row), _resident((1, d)), _resident((d, n)), _resident((1, n)),
                  pl.BlockSpec((tm, LANES), tab), pl.BlockSpec((tm, LANES), tab),
                  pl.BlockSpec((tm, LANES), tab)],
        out_specs=[pl.BlockSpec((PARTS, tm, KV_DIM), parts), pl.BlockSpec((tm, KV_DIM), row),
                   pl.BlockSpec((tm, KV_DIM), row), pl.BlockSpec((PARTS, tm, KV_DIM), parts),
                   pl.BlockSpec((1, WINDOW, KV_DIM), seq), pl.BlockSpec((1, WINDOW, KV_DIM), seq),
                   pl.BlockSpec((1, hist, conv_ch), seq)],
        out_shape=[jax.ShapeDtypeStruct((PARTS, m, KV_DIM), BF16), jax.ShapeDtypeStruct((m, KV_DIM), F32),
                   jax.ShapeDtypeStruct((m, KV_DIM), F32), jax.ShapeDtypeStruct((PARTS, m, KV_DIM), F32),
                   jax.ShapeDtypeStruct((batch, WINDOW, KV_DIM), F32),
                   jax.ShapeDtypeStruct((batch, WINDOW, KV_DIM), F32),
                   jax.ShapeDtypeStruct((batch, hist, conv_ch), F32)],
        compiler_params=_params("arbitrary"),
        name="in_proj",
    )(x, g, w, b, cos, sup, sdn)


def _attn_scores(q, kk, vv, has_prev):
    blk = q.shape[0]
    kk, vv = kk.astype(BF16), vv.astype(BF16)
    lane_group = lax.broadcasted_iota(jnp.int32, kk.shape, 1) // HEAD_DIM
    zero = jnp.zeros_like(kk)
    kstack = jnp.concatenate([jnp.where(lane_group == g, kk, zero) for g in range(N_KV_HEADS)], axis=0)
    vstack = jnp.concatenate([jnp.where(lane_group == g, vv, zero) for g in range(N_KV_HEADS)], axis=0)
    qi = lax.broadcasted_iota(jnp.int32, (blk, blk), 0)
    kj = lax.broadcasted_iota(jnp.int32, (blk, blk), 1)
    from_prev = kj > qi
    return _dot_nt(q, kstack), (from_prev, from_prev & has_prev), vstack


def _attn_probs(s_all, valid, sink_ref, part, zero_tile):
    from_prev, prev_live = valid
    blk = from_prev.shape[1]
    probs = []
    for g in range(N_KV_HEADS):
        s_prev = s_all[:, 2 * g * blk:(2 * g + 1) * blk]
        s_own = s_all[:, (2 * g + 1) * blk:(2 * g + 2) * blk]
        s = jnp.where(from_prev, jnp.where(prev_live, s_prev, -jnp.inf), s_own)
        p = _sink_softmax(s, sink_ref[g * GROUP + part])
        if g == 0:
            p = jnp.concatenate([p[0:SUBLANES, :] + zero_tile, p[SUBLANES:, :]], axis=0)
        zero = jnp.zeros_like(p)
        probs += [jnp.where(from_prev, p, zero).astype(BF16), jnp.where(from_prev, zero, p).astype(BF16)]
    return jnp.concatenate(probs, axis=1)


def _conv_lanes(c, cw_ref, cb_ref, ue_ref, row0, blk, c_out):
    first = CONV_HALO - (CONV_W - 1)
    cs = slice(c * LANES, (c + 1) * LANES)
    seen = None
    for s0 in range(0, blk, CONV_ROWS):
        win = ue_ref[pl.ds(pl.multiple_of(row0 + s0, SUBLANES), CONV_ROWS + CONV_HALO), cs]
        out = cb_ref[:, cs]
        for r in range(SUBLANES):
            rows = CONV_ROWS + (SUBLANES if r else 0)
            part = None
            for a in range((first + CONV_W - 1 - r) // SUBLANES + 1):
                t = a * SUBLANES + r - first
                if 0 <= t < CONV_W:
                    term = cw_ref[t:t + 1, cs] * win[a * SUBLANES:a * SUBLANES + rows, :]
                    part = term if part is None else part + term
            out = out + part[r:r + CONV_ROWS, :]
        c_out[s0:s0 + CONV_ROWS, cs] = out
        for r0 in range(0, CONV_ROWS, SUBLANES):
            tile = out[r0:r0 + SUBLANES, :]
            seen = tile if seen is None else jnp.maximum(seen, tile)
    return seen


def _mlp_mix_kernel(x_ref, q_ref, k_ref, kp_ref, v_ref, vp_ref, u_ref, uh_ref, sink_ref, cw_ref, cb_ref,
                    lg_ref, lb_ref, wo_ref, bo_ref, g2_ref, gf_ref, wu_ref, wd_ref,
                    o_ref, hn_ref, a_s, c_s, kbuf, vbuf, ubuf, *, nb, nrow, parts_per_step, final_norm):
    i, f = pl.program_id(0), pl.program_id(1)
    blk = WINDOW
    blocks_per_tile = x_ref.shape[0] // blk
    steps_per_block = PARTS // parts_per_step

    @pl.when((i == 0) & (f == 0))
    def _():
        a_s[...] = jnp.zeros_like(a_s)
        c_s[...] = jnp.zeros_like(c_s)

    @pl.when(f == 0)
    def _():
        a = jnp.concatenate([a_s[p] for p in range(PARTS)], axis=1)
        conv = jnp.concatenate([c_s[p] for p in range(PARTS)], axis=1)
        c = _layer_norm_swish(conv, lg_ref[...], lb_ref[...]).astype(BF16)
        mix = [ma + mc for ma, mc in zip(_dot_chunks(a, wo_ref, slice(0, Q_DIM)),
                                         _dot_chunks(c, wo_ref, slice(Q_DIM, None)))]
        h = x_ref[...] + jnp.concatenate(mix, axis=1) + bo_ref[...]
        o_ref[...] = h
        hn_ref[...] = _rms_norm_f32(h, g2_ref[...]).astype(hn_ref.dtype)
        kbuf[0:blk, :] = kp_ref[...]
        kbuf[blk:, :] = k_ref[...]
        vbuf[0:blk, :] = vp_ref[...]
        vbuf[blk:, :] = v_ref[...]
        continues = (jnp.minimum(i, nrow - 1) * blocks_per_tile) % nb != 0
        for p in range(PARTS):
            ubuf[p, 0:CONV_HALO, :] = jnp.where(continues, uh_ref[p], jnp.zeros_like(uh_ref[p]))
            ubuf[p, CONV_HALO:, :] = u_ref[p]

    sub, part0 = f // steps_per_block, (f % steps_per_block) * parts_per_step
    parts = [part0 + j for j in range(parts_per_step)]
    gblk = jnp.minimum(i, nrow - 1) * blocks_per_tile + sub
    has_prev = gblk % nb != 0
    rows = pl.ds(pl.multiple_of(sub * blk, blk), blk)
    both = pl.ds(pl.multiple_of(sub * blk, blk), 2 * blk)
    conv_done = []
    for part in parts:
        conv_out = c_s.at[part, rows, :]
        seen = jnp.maximum(_conv_lanes(0, cw_ref.at[part], cb_ref.at[part], ubuf.at[part], sub * blk, blk, conv_out),
                           _conv_lanes(1, cw_ref.at[part], cb_ref.at[part], ubuf.at[part], sub * blk, blk, conv_out))
        conv_done.append(jnp.minimum(jnp.abs(seen), 0.0))
    scores = [_attn_scores(q_ref[part, rows, :], kbuf[both, :], vbuf[both, :], has_prev) for part in parts]
    up = jnp.concatenate(_dot_chunks(hn_ref[...], wu_ref), axis=1)
    act = jnp.square(jnp.maximum(up, 0.0)).astype(BF16)
    for part, (s_all, valid, vstack), zero in zip(parts, scores, conv_done):
        probs = _attn_probs(s_all, valid, sink_ref, part, zero)
        a_s[part, rows, :] = _dot(probs, vstack).astype(a_s.dtype)
    for c, down in enumerate(_dot_chunks(act, wd_ref)):
        o_ref[:, c * COL_CHUNK:(c + 1) * COL_CHUNK] += down

    if final_norm:
        @pl.when(f == pl.num_programs(1) - 1)
        def _():
            o_ref[...] = _rms_norm_f32(o_ref[...], gf_ref[...])


def _mlp_mix(x, q, k, v, u, sinks, conv_w, conv_b, ln_g, ln_b, w_out, b_out, g2, gf, w_up, w_down,
             seq, tm, parts_per_step, final_norm):
    m, d = x.shape
    conv_ch = u.shape[0] * u.shape[2]
    ff = w_down.shape[1]
    blk = WINDOW
    nb = seq // blk
    nrow = m // tm
    bpt = tm // blk
    nf = bpt * PARTS // parts_per_step
    assert ff % (nf * COL_CHUNK) == 0 and conv_ch == PARTS * KV_DIM and Q_DIM == PARTS * KV_DIM
    assert seq % tm == 0
    tf = ff // nf
    hpb = blk // CONV_HALO
    mix_row = lambda i: jnp.minimum(i, nrow - 1)
    mlp_tile = lambda i, f: (jnp.maximum(i - 1, 0), 0)
    mix_tile = lambda i, f: (mix_row(i), 0)
    mix_parts = lambda i, f: (0, mix_row(i), 0)
    prev = lambda i, f: (jnp.maximum(mix_row(i) * bpt - 1, 0), 0)
    halo_parts = lambda i, f: (0, jnp.maximum(mix_row(i) * bpt * hpb - 1, 0), 0)
    return pl.pallas_call(
        functools.partial(_mlp_mix_kernel, nb=nb, nrow=nrow, parts_per_step=parts_per_step,
                          final_norm=final_norm),
        grid=(nrow + 1, nf),
        in_specs=[pl.BlockSpec((tm, d), mlp_tile),
                  pl.BlockSpec((PARTS, tm, KV_DIM), mix_parts),
                  pl.BlockSpec((tm, KV_DIM), mix_tile), pl.BlockSpec((blk, KV_DIM), prev),
                  pl.BlockSpec((tm, KV_DIM), mix_tile), pl.BlockSpec((blk, KV_DIM), prev),
                  pl.BlockSpec((PARTS, tm, KV_DIM), mix_parts),
                  pl.BlockSpec((PARTS, CONV_HALO, KV_DIM), halo_parts),
                  pl.BlockSpec(memory_space=pltpu.SMEM),
                  _resident((PARTS, CONV_W, KV_DIM)), _resident((PARTS, 1, KV_DIM)),
                  _resident((1, conv_ch)), _resident((1, conv_ch)),
                  _resident(w_out.shape), _resident((1, d)), _resident((1, d)), _resident((1, d)),
                  pl.BlockSpec((tf // COL_CHUNK, d, COL_CHUNK), lambda i, f: (f, 0, 0)),
                  pl.BlockSpec((w_down.shape[0], tf, COL_CHUNK), lambda i, f: (0, f, 0))],
        out_specs=pl.BlockSpec((tm, d), mlp_tile),
        out_shape=jax.ShapeDtypeStruct((m, d), F32),
        scratch_shapes=[pltpu.VMEM((tm, d), BF16),
                        pltpu.VMEM((PARTS, tm, KV_DIM), BF16),
                        pltpu.VMEM((PARTS, tm, KV_DIM), F32),
                        pltpu.VMEM((blk + tm, KV_DIM), F32), pltpu.VMEM((blk + tm, KV_DIM), F32),
                        pltpu.VMEM((PARTS, CONV_HALO + tm, KV_DIM), F32)],
        compiler_params=_params("arbitrary", "arbitrary"),
        name="mlp_mix",
    )(x, q, k, k, v, v, u, u, sinks, conv_w, conv_b, ln_g, ln_b, w_out, b_out, g2, gf, w_up, w_down)


def _in_proj_sample_kernel(x_ref, g_ref, w_ref, b_ref, bkv_ref,
                           cos_ref, sup_ref, sdn_ref, cosc_ref, supc_ref, sdnc_ref,
                           q_ref, kt_ref, vt_ref, u_ref):
    conv_ch = u_ref.shape[1]
    o2, o3 = Q_DIM + 2 * KV_DIM, Q_DIM + 2 * KV_DIM + conv_ch
    hn = _rms_norm_f32(x_ref[...], g_ref[...]).astype(BF16)
    zq = _dot(hn, w_ref[:, 0:Q_DIM]) + b_ref[:, 0:Q_DIM]
    q_ref[...] = _rope_lanes(zq, cos_ref[...], sup_ref[...], sdn_ref[...]) * ATTN_SCALE
    zkv = lax.dot_general(w_ref[:, Q_DIM:o2], hn, (((0,), (1,)), ((), ())), preferred_element_type=F32) + bkv_ref[...]
    kt_ref[...] = _rope(zkv[0:KV_DIM, :], cosc_ref[...], supc_ref[...], sdnc_ref[...], 0)
    vt_ref[...] = zkv[KV_DIM:, :]
    zu = _dot(hn, w_ref[:, o2:o3]) + b_ref[:, o2:o3]
    zg = _dot(hn, w_ref[:, o3:]) + b_ref[:, o3:]
    u_ref[...] = zu * jax.nn.sigmoid(zg)


def _in_proj_sample(x, g, w, b, bkv_c, row_tabs, col_tabs):
    n, d = x.shape
    conv_ch = (w.shape[1] - Q_DIM - 2 * KV_DIM) // 2
    full = lambda a: _resident(a.shape)
    args = (x, g, w, b, bkv_c, *row_tabs, *col_tabs)
    return pl.pallas_call(
        _in_proj_sample_kernel,
        grid=(1,),
        in_specs=[full(a) for a in args],
        out_specs=[_resident((n, Q_DIM)), _resident((KV_DIM, n)), _resident((KV_DIM, n)), _resident((n, conv_ch))],
        out_shape=[jax.ShapeDtypeStruct((n, Q_DIM), F32), jax.ShapeDtypeStruct((KV_DIM, n), F32),
                   jax.ShapeDtypeStruct((KV_DIM, n), F32), jax.ShapeDtypeStruct((n, conv_ch), F32)],
        compiler_params=_params("arbitrary"),
        name="in_proj_sample",
    )(*args)


def _attn_sample_kernel(q_ref, kn_ref, vn_ref, ck_ref, cv_ref, sink_ref, ao_ref, nk_ref, nv_ref):
    nseq, _, wb = ck_ref.shape
    rows = N_HEADS
    lane_group = lax.broadcasted_iota(jnp.int32, (rows, KV_DIM), 1) // HEAD_DIM
    row_group = lax.broadcasted_iota(jnp.int32, (rows, KV_DIM), 0) % N_KV_HEADS
    own_lanes = lane_group == row_group
    fold = (lax.broadcasted_iota(jnp.int32, (SUBLANES, rows), 1) // N_KV_HEADS
            == lax.broadcasted_iota(jnp.int32, (SUBLANES, rows), 0)).astype(F32)
    newest = lax.broadcasted_iota(jnp.int32, (KV_DIM, wb), 1) == wb - 1
    sink = sink_ref[...]
    kn, vn = kn_ref[0], vn_ref[0]

    seqs = range(nseq)
    knew = [jnp.where(newest, kn[:, n:n + 1], pltpu.roll(ck_ref[n], wb - 1, 1)) for n in seqs]
    vnew = [jnp.where(newest, vn[:, n:n + 1], pltpu.roll(cv_ref[n], wb - 1, 1)) for n in seqs]
    for n in seqs:
        nk_ref[n] = knew[n]
        nv_ref[n] = vnew[n]
    qrows = [jnp.where(own_lanes, q_ref[n], jnp.zeros((rows, KV_DIM), F32)).astype(BF16) for n in seqs]
    s = [_dot(qrows[n], knew[n].astype(BF16)) for n in seqs]
    p = [_sink_softmax(s[n], sink).astype(BF16) for n in seqs]
    o = [_dot_nt(p[n], vnew[n].astype(BF16)) for n in seqs]
    for n in seqs:
        ao_ref[n] = _dot(fold, jnp.where(own_lanes, o[n], jnp.zeros_like(o[n])))


def _attn_sample(q_rep, kn, vn, cache_kt, cache_vt, sink_rows, nseq):
    n, _, wb = cache_kt.shape
    r3 = lambda s: (s, 0, 0)
    return pl.pallas_call(
        _attn_sample_kernel,
        grid=(n // nseq,),
        in_specs=[pl.BlockSpec((nseq, N_HEADS, KV_DIM), r3),
                  pl.BlockSpec((1, KV_DIM, nseq), r3), pl.BlockSpec((1, KV_DIM, nseq), r3),
                  pl.BlockSpec((nseq, KV_DIM, wb), r3), pl.BlockSpec((nseq, KV_DIM, wb), r3),
                  _resident((N_HEADS, 1))],
        out_specs=[pl.BlockSpec((nseq, SUBLANES, KV_DIM), r3),
                   pl.BlockSpec((nseq, KV_DIM, wb), r3), pl.BlockSpec((nseq, KV_DIM, wb), r3)],
        out_shape=[jax.ShapeDtypeStruct((n, SUBLANES, KV_DIM), F32),
                   jax.ShapeDtypeStruct((n, KV_DIM, wb), F32), jax.ShapeDtypeStruct((n, KV_DIM, wb), F32)],
        compiler_params=_params("arbitrary"),
        name="attn_sample",
    )(q_rep, kn, vn, cache_kt, cache_vt, sink_rows)


def _conv_sample_kernel(cc_ref, cn_ref, u_ref, cw_ref, cb_ref, lg_ref, lb_ref, co_ref, nc_ref, acc_ref, *, hist):
    j = pl.program_id(0)
    taps = cc_ref.shape[0]
    steps = hist // taps
    u = u_ref[...]

    @pl.when(j == 0)
    def _():
        acc_ref[...] = jnp.zeros_like(acc_ref)

    acc = acc_ref[...]
    for t in range(taps):
        acc = acc + cw_ref[j * taps + t] * cc_ref[t]
    acc_ref[...] = acc
    for t in range(taps - 1):
        nc_ref[t] = cc_ref[t + 1]

    @pl.when(j < steps - 1)
    def _():
        nc_ref[taps - 1] = cn_ref[0]

    @pl.when(j == steps - 1)
    def _():
        nc_ref[taps - 1] = u
        y = acc + cw_ref[hist] * u + cb_ref[...]
        co_ref[...] = _layer_norm_swish(y, lg_ref[...], lb_ref[...]).astype(co_ref.dtype)


def _conv_sample(cache_t, u, conv_w, conv_b, ln_g, ln_b, taps):
    hist, n, ch = cache_t.shape
    assert hist % taps == 0
    blk = lambda j: (j, 0, 0)
    return pl.pallas_call(
        functools.partial(_conv_sample_kernel, hist=hist),
        grid=(hist // taps,),
        in_specs=[pl.BlockSpec((taps, n, ch), blk),
                  pl.BlockSpec((1, n, ch), lambda j: (jnp.minimum((j + 1) * taps, hist - 1), 0, 0)),
                  _resident((n, ch)), _resident((hist + 1, 1, ch)),
                  _resident((1, ch)), _resident((1, ch)), _resident((1, ch))],
        out_specs=[_resident((n, ch)), pl.BlockSpec((taps, n, ch), blk)],
        out_shape=[jax.ShapeDtypeStruct((n, ch), BF16), jax.ShapeDtypeStruct((hist, n, ch), F32)],
        scratch_shapes=[pltpu.VMEM((n, ch), F32)],
        compiler_params=_params("arbitrary"),
        name="conv_sample",
    )(cache_t, cache_t, u, conv_w.reshape(hist + 1, 1, ch), conv_b, ln_g, ln_b)


def _out_mlp_kernel(x_ref, a_ref, c_ref, wo_ref, bo_ref, g2_ref, wu_ref, wd_ref, gf_ref,
                    o_ref, hn_ref, *, final_norm):
    f = pl.program_id(1)

    @pl.when(f == 0)
    def _():
        mix = [ma + mc for ma, mc in zip(_dot_chunks(a_ref[...], wo_ref, slice(0, Q_DIM)),
                                         _dot_chunks(c_ref[...], wo_ref, slice(Q_DIM, None)))]
        h = x_ref[...] + jnp.concatenate(mix, axis=1) + bo_ref[...]
        o_ref[...] = h
        hn_ref[...] = _rms_norm_f32(h, g2_ref[...]).astype(hn_ref.dtype)

    up = jnp.concatenate(_dot_chunks(hn_ref[...], wu_ref), axis=1)
    act = jnp.square(jnp.maximum(up, 0.0)).astype(BF16)
    for c, down in enumerate(_dot_chunks(act, wd_ref)):
        o_ref[:, c * COL_CHUNK:(c + 1) * COL_CHUNK] += down

    if final_norm:
        @pl.when(f == pl.num_programs(1) - 1)
        def _():
            o_ref[...] = _rms_norm_f32(o_ref[...], gf_ref[...])


def _out_mlp(x, a, c, w_out, b_out, g2, w_up, w_down, gf, tm, tf, final_norm):
    m, d = x.shape
    conv_ch = w_out.shape[1] - Q_DIM
    ff = w_down.shape[1]
    row = lambda i, f: (i, 0)
    return pl.pallas_call(
        functools.partial(_out_mlp_kernel, final_norm=final_norm),
        grid=(m // tm, ff // tf),
        in_specs=[pl.BlockSpec((tm, d), row), pl.BlockSpec((tm, Q_DIM), row),
                  pl.BlockSpec((tm, conv_ch), row),
                  _resident(w_out.shape), _resident((1, d)), _resident((1, d)),
                  pl.BlockSpec((tf // COL_CHUNK, d, COL_CHUNK), lambda i, f: (f, 0, 0)),
                  pl.BlockSpec((w_down.shape[0], tf, COL_CHUNK), lambda i, f: (0, f, 0)),
                  _resident((1, d))],
        out_specs=pl.BlockSpec((tm, d), row),
        out_shape=jax.ShapeDtypeStruct((m, d), F32),
        scratch_shapes=[pltpu.VMEM((tm, d), BF16)],
        compiler_params=_params("arbitrary", "arbitrary"),
        name="out_mlp",
    )(x, a, c, w_out, b_out, g2, w_up, w_down, gf)


def _rope_tables(pos):
    half = ROT_DIM // 2
    inv_freq = jnp.power(jnp.float32(ROPE_THETA), -jnp.arange(half, dtype=jnp.float32) * 2.0 / ROT_DIM)
    ang = pos.astype(jnp.float32)[:, None] * inv_freq[None, :]
    cos, sin = jnp.cos(ang), jnp.sin(ang)
    t = pos.shape[0]
    pad = jnp.zeros((t, HEAD_DIM - ROT_DIM), F32)
    zeros = jnp.zeros((t, half), F32)
    per_head = lambda a, b, fill: jnp.concatenate([a, b, pad + fill], axis=1)
    reps = LANES // HEAD_DIM
    cos_t = jnp.tile(per_head(cos, cos, 1.0), (1, reps))
    sup_t = jnp.tile(per_head(zeros, sin, 0.0), (1, reps))
    sdn_t = jnp.tile(per_head(-sin, zeros, 0.0), (1, reps))
    return cos_t, sup_t, sdn_t


def _heads_to_igd(a, axis):
    shape = a.shape
    a = a.reshape(shape[:axis] + (N_KV_HEADS, GROUP, HEAD_DIM) + shape[axis + 1:])
    return jnp.swapaxes(a, axis, axis + 1).reshape(shape)


def _cast_slabs_kernel(w_ref, o_ref):
    o_ref[0] = w_ref[...].astype(o_ref.dtype)


def _cast_slabs(w, tk):
    k, n = w.shape
    return pl.pallas_call(
        _cast_slabs_kernel,
        grid=(n // COL_CHUNK, k // tk),
        in_specs=[pl.BlockSpec((tk, COL_CHUNK), lambda j, kk: (kk, j))],
        out_specs=pl.BlockSpec((1, tk, COL_CHUNK), lambda j, kk: (j, kk, 0)),
        out_shape=jax.ShapeDtypeStruct((n // COL_CHUNK, k, COL_CHUNK), BF16),
        compiler_params=_params("arbitrary", "arbitrary"),
        name="cast_slabs",
    )(w)


def _igd_source(idx):
    i, g, d = idx // KV_DIM, (idx // HEAD_DIM) % N_KV_HEADS, idx % HEAD_DIM
    return (g * GROUP + i) * HEAD_DIM + d


def _prep_w_in_kernel(w_ref, o_ref):
    w = w_ref[...].astype(o_ref.dtype)
    src = lax.broadcasted_iota(jnp.int32, (Q_DIM, Q_DIM), 0)
    dst = lax.broadcasted_iota(jnp.int32, (Q_DIM, Q_DIM), 1)
    move = jnp.where(src == _igd_source(dst), 1.0, 0.0).astype(o_ref.dtype)
    o_ref[:, 0:Q_DIM] = _dot(w[:, 0:Q_DIM], move).astype(o_ref.dtype)
    o_ref[:, Q_DIM:] = w[:, Q_DIM:]


def _prep_w_in(w, tk):
    k, n = w.shape
    return pl.pallas_call(
        _prep_w_in_kernel,
        grid=(k // tk,),
        in_specs=[pl.BlockSpec((tk, n), lambda j: (j, 0))],
        out_specs=pl.BlockSpec((tk, n), lambda j: (j, 0)),
        out_shape=jax.ShapeDtypeStruct((k, n), BF16),
        compiler_params=_params("arbitrary"),
        name="prep_w_in",
    )(w)


def _prep_w_out_kernel(w_ref, o_ref):
    w = w_ref[...].astype(o_ref.dtype)
    dst = lax.broadcasted_iota(jnp.int32, (Q_DIM, Q_DIM), 0)
    src = lax.broadcasted_iota(jnp.int32, (Q_DIM, Q_DIM), 1)
    move = jnp.where(src == _igd_source(dst), 1.0, 0.0).astype(o_ref.dtype)
    o_ref[0, 0:Q_DIM, :] = _dot(move, w[0:Q_DIM, :]).astype(o_ref.dtype)
    o_ref[0, Q_DIM:, :] = w[Q_DIM:, :]


def _prep_w_out(w):
    k, n = w.shape
    return pl.pallas_call(
        _prep_w_out_kernel,
        grid=(n // COL_CHUNK,),
        in_specs=[pl.BlockSpec((k, COL_CHUNK), lambda c: (0, c))],
        out_specs=pl.BlockSpec((1, k, COL_CHUNK), lambda c: (c, 0, 0)),
        out_shape=jax.ShapeDtypeStruct((n // COL_CHUNK, k, COL_CHUNK), BF16),
        compiler_params=_params("arbitrary"),
        name="prep_w_out",
    )(w)


def _tile(m, target):
    t = min(m, target)
    assert m % t == 0, (m, t)
    return t


def kernel(x_prompt, x_sample, cache_k, cache_v, cache_conv, norm1_g, w_in, b_in, attn_sinks,
           conv_w, conv_b, conv_ln_g, conv_ln_b, w_out, b_out, norm2_g, w_up, w_down, final_norm_g):
    batch, seq, d_model = x_prompt.shape
    nsamp, t_s, _ = x_sample.shape
    depth = w_in.shape[0]
    conv_ch = conv_w.shape[2]
    wb = cache_k.shape[2]
    hist = CONV_W - 1
    assert t_s == 1 and seq % WINDOW == 0 and wb == WINDOW and cache_conv.shape[2] == hist
    assert depth >= 1

    tabs_p = _rope_tables(jnp.arange(seq, dtype=jnp.int32))
    tabs_s1 = _rope_tables(PAST_LEN + jnp.arange(t_s, dtype=jnp.int32))
    tabs_s = tuple(jnp.broadcast_to(t, (nsamp, LANES)) for t in tabs_s1)
    tabs_sc = tuple(jnp.tile(t[0], KV_DIM // LANES).reshape(KV_DIM, 1) for t in tabs_s1)
    row = lambda a: a.reshape(1, -1)

    hp = x_prompt.reshape(batch * seq, d_model)
    hs = x_sample.reshape(nsamp * t_s, d_model)
    tm_in = _tile(seq, 512)
    tm_mlp = _tile(batch * seq, 512)
    tf_s = 2 * COL_CHUNK
    nseq = _tile(nsamp, 8)
    o1, o2 = Q_DIM, Q_DIM + 2 * KV_DIM
    pk, pv, pc, sk, sv, sc = [], [], [], [], [], []
    for l in range(depth):
        last = l == depth - 1
        w_in_l = _prep_w_in(w_in[l], _tile(d_model, 512))
        b_in_l = row(jnp.concatenate([_heads_to_igd(b_in[l][:o1], 0), b_in[l][o1:]]))
        w_out_l = _prep_w_out(w_out[l])
        w_up_l, w_down_l = _cast_slabs(w_up[l], _tile(d_model, 2048)), _cast_slabs(w_down[l], 2048)
        sinks = attn_sinks[l].astype(F32)
        sink_rows = sinks.reshape(N_KV_HEADS, GROUP).T.reshape(N_HEADS, 1)
        g1, g2, gf = row(norm1_g[l]), row(norm2_g[l]), row(final_norm_g)
        cb, lg, lb, bo = row(conv_b[l]), row(conv_ln_g[l]), row(conv_ln_b[l]), row(b_out[l])

        q, k, v, u, pk_l, pv_l, pc_l = _in_proj(hp, g1, w_in_l, b_in_l, *tabs_p, tm_in, batch)
        by_part = lambda a: a.reshape(a.shape[0], PARTS, KV_DIM).transpose(1, 0, 2)
        hp = _mlp_mix(hp, q, k, v, u, sinks, by_part(conv_w[l]), by_part(cb), lg, lb, w_out_l, bo, g2, gf,
                      w_up_l, w_down_l, seq, tm_mlp, 1, last)
        pk.append(pk_l.reshape(batch, WINDOW, N_KV_HEADS, HEAD_DIM))
        pv.append(pv_l.reshape(batch, WINDOW, N_KV_HEADS, HEAD_DIM))
        pc.append(pc_l)

        cache_kt = jnp.transpose(cache_k[l], (0, 2, 3, 1)).reshape(nsamp, KV_DIM, wb)
        cache_vt = jnp.transpose(cache_v[l], (0, 2, 3, 1)).reshape(nsamp, KV_DIM, wb)
        cache_ct = jnp.transpose(cache_conv[l], (1, 0, 2))
        bkv_c = b_in[l][o1:o2].reshape(2 * KV_DIM, 1)
        q, kt, vt, u = _in_proj_sample(hs, g1, w_in_l, b_in_l, bkv_c, tabs_s, tabs_sc)
        q_rep = jnp.repeat(q.reshape(nsamp, GROUP, KV_DIM), N_KV_HEADS, axis=1)
        by_block = lambda a: a.reshape(KV_DIM, nsamp // nseq, nseq).transpose(1, 0, 2)
        a8, nkt, nvt = _attn_sample(q_rep, by_block(kt), by_block(vt), cache_kt, cache_vt, sink_rows, nseq)
        c_o, nct = _conv_sample(cache_ct, u, conv_w[l], cb, lg, lb, 5)
        a_o = a8[:, :GROUP, :].reshape(nsamp, Q_DIM).astype(BF16)
        hs = _out_mlp(hs, a_o, c_o, w_out_l, bo, g2, w_up_l, w_down_l, gf, nsamp, tf_s, last)
        to_cache = lambda a: jnp.transpose(a.reshape(nsamp, N_KV_HEADS, HEAD_DIM, wb), (0, 3, 1, 2))
        sk.append(to_cache(nkt))
        sv.append(to_cache(nvt))
        sc.append(jnp.transpose(nct, (1, 0, 2)))

    y_prompt = hp.reshape(batch, seq, d_model)
    y_sample = hs.reshape(nsamp, t_s, d_model)
    return (y_prompt, y_sample, jnp.stack(pk), jnp.stack(pv), jnp.stack(pc),
            jnp.stack(sk), jnp.stack(sv), jnp.stack(sc))
```

```python
import functools

import jax
import jax.numpy as jnp
import numpy as np
from jax import lax
from jax.experimental import pallas as pl
from jax.experimental.pallas import tpu as pltpu

HEAD_DIM = 64
N_HEADS = 16
N_KV_HEADS = 4
GROUP = N_HEADS // N_KV_HEADS
WINDOW = 128
ROT_DIM = HEAD_DIM // 4
ROPE_THETA = 500000.0
ATTN_SCALE = HEAD_DIM ** -0.5
Q_DIM = N_HEADS * HEAD_DIM
KV_DIM = N_KV_HEADS * HEAD_DIM
CONV_W = 31
PAST_LEN = 16384
EPS = 1e-5

LANES = 128
SUBLANES = 8
CONV_HALO = 32
CONV_ROWS = 64
VMEM_LIMIT_BYTES = 60 * 1024 * 1024
PARTS = GROUP
COL_CHUNK = 512

BF16 = jnp.bfloat16
F32 = jnp.float32


def _resident(shape):
    return pl.BlockSpec(shape, lambda *_: (0,) * len(shape), pipeline_mode=pl.Buffered(1))


def _params(*semantics):
    return pltpu.CompilerParams(dimension_semantics=semantics, vmem_limit_bytes=VMEM_LIMIT_BYTES)


def _rms_norm_f32(x, g):
    return x * lax.rsqrt(jnp.mean(x * x, axis=-1, keepdims=True) + EPS) * g


def _layer_norm_swish(y, g, b):
    mu = jnp.mean(y, axis=-1, keepdims=True)
    yc = y - mu
    var = jnp.mean(yc * yc, axis=-1, keepdims=True)
    yn = yc * lax.rsqrt(var + EPS) * g + b
    return yn * jax.nn.sigmoid(yn)


def _dot(a, b):
    return jnp.dot(a, b, preferred_element_type=F32)


def _dot_nt(a, b):
    return lax.dot_general(a, b, (((1,), (1,)), ((), ())), preferred_element_type=F32)


def _dot_chunks(a, w_ref, rows=slice(None)):
    return [_dot(a, w_ref[c, rows, :]) for c in range(w_ref.shape[0])]


def _sink_softmax(s, sink):
    m = jnp.maximum(jnp.max(s, axis=-1, keepdims=True), sink)
    p = jnp.exp(s - m)
    return p / (jnp.sum(p, axis=-1, keepdims=True) + jnp.exp(sink - m))


def _rope(z, cos, sin_up, sin_dn, axis):
    n = z.shape[axis]
    half = ROT_DIM // 2
    return z * cos + pltpu.roll(z, half, axis) * sin_up + pltpu.roll(z, n - half, axis) * sin_dn


def _rope_lanes(z, cos, sin_up, sin_dn):
    cols = [_rope(z[:, c * LANES:(c + 1) * LANES], cos, sin_up, sin_dn, 1) for c in range(z.shape[1] // LANES)]
    return jnp.concatenate(cols, axis=1)


def _in_proj_kernel(x_ref, g_ref, w_ref, b_ref, cos_ref, sup_ref, sdn_ref,
                    q_ref, k_ref, v_ref, u_ref, pk_ref, pv_ref, pc_ref):
    conv_ch = u_ref.shape[0] * u_ref.shape[2]
    tm = x_ref.shape[0]
    hist = pc_ref.shape[1]
    o1, o2, o3, o4 = Q_DIM, Q_DIM + KV_DIM, Q_DIM + 2 * KV_DIM, Q_DIM + 2 * KV_DIM + conv_ch
    hn = _rms_norm_f32(x_ref[...], g_ref[...]).astype(BF16)
    cos, sup, sdn = cos_ref[...], sup_ref[...], sdn_ref[...]

    zq = _dot(hn, w_ref[:, 0:o1]) + b_ref[:, 0:o1]
    q = (_rope_lanes(zq, cos, sup, sdn) * ATTN_SCALE).astype(q_ref.dtype)
    zk = _dot(hn, w_ref[:, o1:o2]) + b_ref[:, o1:o2]
    k_ref[...] = _rope_lanes(zk, cos, sup, sdn)
    v_ref[...] = _dot(hn, w_ref[:, o2:o3]) + b_ref[:, o2:o3]
    zu = _dot(hn, w_ref[:, o3:o4]) + b_ref[:, o3:o4]
    zg = _dot(hn, w_ref[:, o4:]) + b_ref[:, o4:]
    u = zu * jax.nn.sigmoid(zg)
    for p in range(PARTS):
        q_ref[p] = q[:, p * KV_DIM:(p + 1) * KV_DIM]
        u_ref[p] = u[:, p * KV_DIM:(p + 1) * KV_DIM]
    pk_ref[0] = k_ref[tm - WINDOW:, :]
    pv_ref[0] = v_ref[tm - WINDOW:, :]
    pc_ref[0] = u[tm - hist:, :]


def _in_proj(x, g, w, b, cos, sup, sdn, tm, batch):
    m, d = x.shape
    n = w.shape[1]
    conv_ch = (n - Q_DIM - 2 * KV_DIM) // 2
    nt = cos.shape[0] // tm
    hist = CONV_W - 1
    row = lambda i: (i, 0)
    parts = lambda i: (0, i, 0)
    tab = lambda i: (i % nt, 0)
    seq = lambda i: (i // nt, 0, 0)
    assert conv_ch == PARTS * KV_DIM and Q_DIM == PARTS * KV_DIM
    return pl.pallas_call(
        _in_proj_kernel,
        grid=(m // tm,),
        in_specs=[pl.BlockSpec((tm, d), row), _resident((1, d)), _resident((d, n)), _resident((1, n)),
                  pl.BlockSpec((tm, LANES), tab), pl.BlockSpec((tm, LANES), tab),
                  pl.BlockSpec((tm, LANES), tab)],
        out_specs=[pl.BlockSpec((PARTS, tm, KV_DIM), parts), pl.BlockSpec((tm, KV_DIM), row),
                   pl.BlockSpec((tm, KV_DIM), row), pl.BlockSpec((PARTS, tm, KV_DIM), parts),
                   pl.BlockSpec((1, WINDOW, KV_DIM), seq), pl.BlockSpec((1, WINDOW, KV_DIM), seq),
                   pl.BlockSpec((1, hist, conv_ch), seq)],
        out_shape=[jax.ShapeDtypeStruct((PARTS, m, KV_DIM), BF16), jax.ShapeDtypeStruct((m, KV_DIM), F32),
                   jax.ShapeDtypeStruct((m, KV_DIM), F32), jax.ShapeDtypeStruct((PARTS, m, KV_DIM), F32),
                   jax.ShapeDtypeStruct((batch, WINDOW, KV_DIM), F32),
                   jax.ShapeDtypeStruct((batch, WINDOW, KV_DIM), F32),
                   jax.ShapeDtypeStruct((batch, hist, conv_ch), F32)],
        compiler_params=_params("arbitrary"),
        name="in_proj",
    )(x, g, w, b, cos, sup, sdn)


def _group_stack(x):
    x = x.astype(BF16)
    lane_group = lax.broadcasted_iota(jnp.int32, x.shape, 1) // HEAD_DIM
    zero = jnp.zeros_like(x)
    return jnp.concatenate([jnp.where(lane_group == g, x, zero) for g in range(N_KV_HEADS)], axis=0)


def _attn_scores(q, kstack, vstack, has_prev):
    blk = q.shape[0]
    qi = lax.broadcasted_iota(jnp.int32, (blk, blk), 0)
    kj = lax.broadcasted_iota(jnp.int32, (blk, blk), 1)
    from_prev = kj > qi
    return _dot_nt(q, kstack), (from_prev, from_prev & has_prev), vstack


def _attn_probs(s_all, valid, sink_ref, part, zero_tile):
    from_prev, prev_live = valid
    blk = from_prev.shape[1]
    probs = []
    for g in range(N_KV_HEADS):
        s_prev = s_all[:, 2 * g * blk:(2 * g + 1) * blk]
        s_own = s_all[:, (2 * g + 1) * blk:(2 * g + 2) * blk]
        s = jnp.where(from_prev, jnp.where(prev_live, s_prev, -jnp.inf), s_own)
        p = _sink_softmax(s, sink_ref[g * GROUP + part])
        if g == 0:
            p = jnp.concatenate([p[0:SUBLANES, :] + zero_tile, p[SUBLANES:, :]], axis=0)
        zero = jnp.zeros_like(p)
        probs += [jnp.where(from_prev, p, zero).astype(BF16), jnp.where(from_prev, zero, p).astype(BF16)]
    return jnp.concatenate(probs, axis=1)


def _conv_lanes(c, cw_ref, cb_ref, ue_ref, row0, blk, c_out):
    first = CONV_HALO - (CONV_W - 1)
    cs = slice(c * LANES, (c + 1) * LANES)
    seen = None
    for s0 in range(0, blk, CONV_ROWS):
        win = ue_ref[pl.ds(pl.multiple_of(row0 + s0, SUBLANES), CONV_ROWS + CONV_HALO), cs]
        out = cb_ref[:, cs]
        for r in range(SUBLANES):
            rows = CONV_ROWS + (SUBLANES if r else 0)
            part = None
            for a in range((first + CONV_W - 1 - r) // SUBLANES + 1):
                t = a * SUBLANES + r - first
                if 0 <= t < CONV_W:
                    term = cw_ref[t:t + 1, cs] * win[a * SUBLANES:a * SUBLANES + rows, :]
                    part = term if part is None else part + term
            out = out + part[r:r + CONV_ROWS, :]
        c_out[s0:s0 + CONV_ROWS, cs] = out
        for r0 in range(0, CONV_ROWS, SUBLANES):
            tile = out[r0:r0 + SUBLANES, :]
            seen = tile if seen is None else jnp.maximum(seen, tile)
    return seen


def _mlp_mix_kernel(x_ref, q_ref, k_ref, kp_ref, v_ref, vp_ref, u_ref, uh_ref, sink_ref, cw_ref, cb_ref,
                    lg_ref, lb_ref, wo_ref, bo_ref, g2_ref, gf_ref, wu_ref, wd_ref,
                    o_ref, hn_ref, a_s, c_s, kstk, vstk, ubuf, *, nb, nrow, parts_per_step, final_norm):
    i, f = pl.program_id(0), pl.program_id(1)
    blk = WINDOW
    blocks_per_tile = x_ref.shape[0] // blk
    steps_per_block = PARTS // parts_per_step

    @pl.when((i == 0) & (f == 0))
    def _():
        a_s[...] = jnp.zeros_like(a_s)
        c_s[...] = jnp.zeros_like(c_s)

    @pl.when(f == 0)
    def _():
        a = jnp.concatenate([a_s[p] for p in range(PARTS)], axis=1)
        conv = jnp.concatenate([c_s[p] for p in range(PARTS)], axis=1)
        c = _layer_norm_swish(conv, lg_ref[...], lb_ref[...]).astype(BF16)
        mix = [ma + mc for ma, mc in zip(_dot_chunks(a, wo_ref, slice(0, Q_DIM)),
                                         _dot_chunks(c, wo_ref, slice(Q_DIM, None)))]
        h = x_ref[...] + jnp.concatenate(mix, axis=1) + bo_ref[...]
        o_ref[...] = h
        hn_ref[...] = _rms_norm_f32(h, g2_ref[...]).astype(hn_ref.dtype)
        for b in range(blocks_per_tile):
            lo = (b - 1) * blk
            kstk[b] = _group_stack(k_ref[lo:lo + 2 * blk, :] if b else
                                   jnp.concatenate([kp_ref[...], k_ref[0:blk, :]], axis=0))
            vstk[b] = _group_stack(v_ref[lo:lo + 2 * blk, :] if b else
                                   jnp.concatenate([vp_ref[...], v_ref[0:blk, :]], axis=0))
        continues = (jnp.minimum(i, nrow - 1) * blocks_per_tile) % nb != 0
        for p in range(PARTS):
            ubuf[p, 0:CONV_HALO, :] = jnp.where(continues, uh_ref[p], jnp.zeros_like(uh_ref[p]))
            ubuf[p, CONV_HALO:, :] = u_ref[p]

    sub, part0 = f // steps_per_block, (f % steps_per_block) * parts_per_step
    parts = [part0 + j for j in range(parts_per_step)]
    gblk = jnp.minimum(i, nrow - 1) * blocks_per_tile + sub
    has_prev = gblk % nb != 0
    rows = pl.ds(pl.multiple_of(sub * blk, blk), blk)
    conv_done = []
    for part in parts:
        conv_out = c_s.at[part, rows, :]
        seen = jnp.maximum(_conv_lanes(0, cw_ref.at[part], cb_ref.at[part], ubuf.at[part], sub * blk, blk, conv_out),
                           _conv_lanes(1, cw_ref.at[part], cb_ref.at[part], ubuf.at[part], sub * blk, blk, conv_out))
        conv_done.append(jnp.minimum(jnp.abs(seen), 0.0))
    scores = [_attn_scores(q_ref[part, rows, :], kstk[sub], vstk[sub], has_prev) for part in parts]
    up = jnp.concatenate(_dot_chunks(hn_ref[...], wu_ref), axis=1)
    act = jnp.square(jnp.maximum(up, 0.0)).astype(BF16)
    for part, (s_all, valid, vstack), zero in zip(parts, scores, conv_done):
        probs = _attn_probs(s_all, valid, sink_ref, part, zero)
        a_s[part, rows, :] = _dot(probs, vstack).astype(a_s.dtype)
    for c, down in enumerate(_dot_chunks(act, wd_ref)):
        o_ref[:, c * COL_CHUNK:(c + 1) * COL_CHUNK] += down

    if final_norm:
        @pl.when(f == pl.num_programs(1) - 1)
        def _():
            o_ref[...] = _rms_norm_f32(o_ref[...], gf_ref[...])


def _mlp_mix(x, q, k, v, u, sinks, conv_w, conv_b, ln_g, ln_b, w_out, b_out, g2, gf, w_up, w_down,
             seq, tm, parts_per_step, final_norm):
    m, d = x.shape
    conv_ch = u.shape[0] * u.shape[2]
    ff = w_down.shape[1]
    blk = WINDOW
    nb = seq // blk
    nrow = m // tm
    bpt = tm // blk
    nf = bpt * PARTS // parts_per_step
    assert ff % (nf * COL_CHUNK) == 0 and conv_ch == PARTS * KV_DIM and Q_DIM == PARTS * KV_DIM
    assert seq % tm == 0
    tf = ff // nf
    hpb = blk // CONV_HALO
    mix_row = lambda i: jnp.minimum(i, nrow - 1)
    mlp_tile = lambda i, f: (jnp.maximum(i - 1, 0), 0)
    mix_tile = lambda i, f: (mix_row(i), 0)
    mix_parts = lambda i, f: (0, mix_row(i), 0)
    prev = lambda i, f: (jnp.maximum(mix_row(i) * bpt - 1, 0), 0)
    halo_parts = lambda i, f: (0, jnp.maximum(mix_row(i) * bpt * hpb - 1, 0), 0)
    return pl.pallas_call(
        functools.partial(_mlp_mix_kernel, nb=nb, nrow=nrow, parts_per_step=parts_per_step,
                          final_norm=final_norm),
        grid=(nrow + 1, nf),
        in_specs=[pl.BlockSpec((tm, d), mlp_tile),
                  pl.BlockSpec((PARTS, tm, KV_DIM), mix_parts),
                  pl.BlockSpec((tm, KV_DIM), mix_tile), pl.BlockSpec((blk, KV_DIM), prev),
                  pl.BlockSpec((tm, KV_DIM), mix_tile), pl.BlockSpec((blk, KV_DIM), prev),
                  pl.BlockSpec((PARTS, tm, KV_DIM), mix_parts),
                  pl.BlockSpec((PARTS, CONV_HALO, KV_DIM), halo_parts),
                  pl.BlockSpec(memory_space=pltpu.SMEM),
                  _resident((PARTS, CONV_W, KV_DIM)), _resident((PARTS, 1, KV_DIM)),
                  _resident((1, conv_ch)), _resident((1, conv_ch)),
                  _resident(w_out.shape), _resident((1, d)), _resident((1, d)), _resident((1, d)),
                  pl.BlockSpec((tf // COL_CHUNK, d, COL_CHUNK), lambda i, f: (f, 0, 0)),
                  pl.BlockSpec((w_down.shape[0], tf, COL_CHUNK), lambda i, f: (0, f, 0))],
        out_specs=pl.BlockSpec((tm, d), mlp_tile),
        out_shape=jax.ShapeDtypeStruct((m, d), F32),
        scratch_shapes=[pltpu.VMEM((tm, d), BF16),
                        pltpu.VMEM((PARTS, tm, KV_DIM), BF16),
                        pltpu.VMEM((PARTS, tm, KV_DIM), F32),
                        pltpu.VMEM((bpt, N_KV_HEADS * 2 * blk, KV_DIM), BF16),
                        pltpu.VMEM((bpt, N_KV_HEADS * 2 * blk, KV_DIM), BF16),
                        pltpu.VMEM((PARTS, CONV_HALO + tm, KV_DIM), F32)],
        compiler_params=_params("arbitrary", "arbitrary"),
        name="mlp_mix",
    )(x, q, k, k, v, v, u, u, sinks, conv_w, conv_b, ln_g, ln_b, w_out, b_out, g2, gf, w_up, w_down)


def _in_proj_sample_kernel(x_ref, g_ref, w_ref, b_ref, bkv_ref,
                           cos_ref, sup_ref, sdn_ref, cosc_ref, supc_ref, sdnc_ref,
                           q_ref, kt_ref, vt_ref, u_ref):
    conv_ch = u_ref.shape[1]
    o2, o3 = Q_DIM + 2 * KV_DIM, Q_DIM + 2 * KV_DIM + conv_ch
    hn = _rms_norm_f32(x_ref[...], g_ref[...]).astype(BF16)
    zq = _dot(hn, w_ref[:, 0:Q_DIM]) + b_ref[:, 0:Q_DIM]
    q_ref[...] = _rope_lanes(zq, cos_ref[...], sup_ref[...], sdn_ref[...]) * ATTN_SCALE
    zkv = lax.dot_general(w_ref[:, Q_DIM:o2], hn, (((0,), (1,)), ((), ())), preferred_element_type=F32) + bkv_ref[...]
    kt_ref[...] = _rope(zkv[0:KV_DIM, :], cosc_ref[...], supc_ref[...], sdnc_ref[...], 0)
    vt_ref[...] = zkv[KV_DIM:, :]
    zu = _dot(hn, w_ref[:, o2:o3]) + b_ref[:, o2:o3]
    zg = _dot(hn, w_ref[:, o3:]) + b_ref[:, o3:]
    u_ref[...] = zu * jax.nn.sigmoid(zg)


def _in_proj_sample(x, g, w, b, bkv_c, row_tabs, col_tabs):
    n, d = x.shape
    conv_ch = (w.shape[1] - Q_DIM - 2 * KV_DIM) // 2
    full = lambda a: _resident(a.shape)
    args = (x, g, w, b, bkv_c, *row_tabs, *col_tabs)
    return pl.pallas_call(
        _in_proj_sample_kernel,
        grid=(1,),
        in_specs=[full(a) for a in args],
        out_specs=[_resident((n, Q_DIM)), _resident((KV_DIM, n)), _resident((KV_DIM, n)), _resident((n, conv_ch))],
        out_shape=[jax.ShapeDtypeStruct((n, Q_DIM), F32), jax.ShapeDtypeStruct((KV_DIM, n), F32),
                   jax.ShapeDtypeStruct((KV_DIM, n), F32), jax.ShapeDtypeStruct((n, conv_ch), F32)],
        compiler_params=_params("arbitrary"),
        name="in_proj_sample",
    )(*args)


def _attn_sample_kernel(q_ref, kn_ref, vn_ref, ck_ref, cv_ref, sink_ref, ao_ref, nk_ref, nv_ref):
    nseq, _, wb = ck_ref.shape
    rows = N_HEADS
    lane_group = lax.broadcasted_iota(jnp.int32, (rows, KV_DIM), 1) // HEAD_DIM
    row_group = lax.broadcasted_iota(jnp.int32, (rows, KV_DIM), 0) % N_KV_HEADS
    own_lanes = lane_group == row_group
    fold = (lax.broadcasted_iota(jnp.int32, (SUBLANES, rows), 1) // N_KV_HEADS
            == lax.broadcasted_iota(jnp.int32, (SUBLANES, rows), 0)).astype(F32)
    newest = lax.broadcasted_iota(jnp.int32, (KV_DIM, wb), 1) == wb - 1
    sink = sink_ref[...]
    kn, vn = kn_ref[0], vn_ref[0]

    seqs = range(nseq)
    knew = [jnp.where(newest, kn[:, n:n + 1], pltpu.roll(ck_ref[n], wb - 1, 1)) for n in seqs]
    vnew = [jnp.where(newest, vn[:, n:n + 1], pltpu.roll(cv_ref[n], wb - 1, 1)) for n in seqs]
    for n in seqs:
        nk_ref[n] = knew[n]
        nv_ref[n] = vnew[n]
    qrows = [jnp.where(own_lanes, q_ref[n], jnp.zeros((rows, KV_DIM), F32)).astype(BF16) for n in seqs]
    s = [_dot(qrows[n], knew[n].astype(BF16)) for n in seqs]
    p = [_sink_softmax(s[n], sink).astype(BF16) for n in seqs]
    o = [_dot_nt(p[n], vnew[n].astype(BF16)) for n in seqs]
    for n in seqs:
        ao_ref[n] = _dot(fold, jnp.where(own_lanes, o[n], jnp.zeros_like(o[n])))


def _attn_sample(q_rep, kn, vn, cache_kt, cache_vt, sink_rows, nseq):
    n, _, wb = cache_kt.shape
    r3 = lambda s: (s, 0, 0)
    return pl.pallas_call(
        _attn_sample_kernel,
        grid=(n // nseq,),
        in_specs=[pl.BlockSpec((nseq, N_HEADS, KV_DIM), r3),
                  pl.BlockSpec((1, KV_DIM, nseq), r3), pl.BlockSpec((1, KV_DIM, nseq), r3),
                  pl.BlockSpec((nseq, KV_DIM, wb), r3), pl.BlockSpec((nseq, KV_DIM, wb), r3),
                  _resident((N_HEADS, 1))],
        out_specs=[pl.BlockSpec((nseq, SUBLANES, KV_DIM), r3),
                   pl.BlockSpec((nseq, KV_DIM, wb), r3), pl.BlockSpec((nseq, KV_DIM, wb), r3)],
        out_shape=[jax.ShapeDtypeStruct((n, SUBLANES, KV_DIM), F32),
                   jax.ShapeDtypeStruct((n, KV_DIM, wb), F32), jax.ShapeDtypeStruct((n, KV_DIM, wb), F32)],
        compiler_params=_params("arbitrary"),
        name="attn_sample",
    )(q_rep, kn, vn, cache_kt, cache_vt, sink_rows)


def _conv_sample_kernel(cc_ref, cn_ref, u_ref, cw_ref, cb_ref, lg_ref, lb_ref, co_ref, nc_ref, acc_ref, *, hist):
    j = pl.program_id(0)
    taps = cc_ref.shape[0]
    steps = hist // taps
    u = u_ref[...]

    @pl.when(j == 0)
    def _():
        acc_ref[...] = jnp.zeros_like(acc_ref)

    acc = acc_ref[...]
    for t in range(taps):
        acc = acc + cw_ref[j * taps + t] * cc_ref[t]
    acc_ref[...] = acc
    for t in range(taps - 1):
        nc_ref[t] = cc_ref[t + 1]

    @pl.when(j < steps - 1)
    def _():
        nc_ref[taps - 1] = cn_ref[0]

    @pl.when(j == steps - 1)
    def _():
        nc_ref[taps - 1] = u
        y = acc + cw_ref[hist] * u + cb_ref[...]
        co_ref[...] = _layer_norm_swish(y, lg_ref[...], lb_ref[...]).astype(co_ref.dtype)


def _conv_sample(cache_t, u, conv_w, conv_b, ln_g, ln_b, taps):
    hist, n, ch = cache_t.shape
    assert hist % taps == 0
    blk = lambda j: (j, 0, 0)
    return pl.pallas_call(
        functools.partial(_conv_sample_kernel, hist=hist),
        grid=(hist // taps,),
        in_specs=[pl.BlockSpec((taps, n, ch), blk),
                  pl.BlockSpec((1, n, ch), lambda j: (jnp.minimum((j + 1) * taps, hist - 1), 0, 0)),
                  _resident((n, ch)), _resident((hist + 1, 1, ch)),
                  _resident((1, ch)), _resident((1, ch)), _resident((1, ch))],
        out_specs=[_resident((n, ch)), pl.BlockSpec((taps, n, ch), blk)],
        out_shape=[jax.ShapeDtypeStruct((n, ch), BF16), jax.ShapeDtypeStruct((hist, n, ch), F32)],
        scratch_shapes=[pltpu.VMEM((n, ch), F32)],
        compiler_params=_params("arbitrary"),
        name="conv_sample",
    )(cache_t, cache_t, u, conv_w.reshape(hist + 1, 1, ch), conv_b, ln_g, ln_b)


def _out_mlp_kernel(x_ref, a_ref, c_ref, wo_ref, bo_ref, g2_ref, wu_ref, wd_ref, gf_ref,
                    o_ref, hn_ref, *, final_norm):
    f = pl.program_id(1)

    @pl.when(f == 0)
    def _():
        mix = [ma + mc for ma, mc in zip(_dot_chunks(a_ref[...], wo_ref, slice(0, Q_DIM)),
                                         _dot_chunks(c_ref[...], wo_ref, slice(Q_DIM, None)))]
        h = x_ref[...] + jnp.concatenate(mix, axis=1) + bo_ref[...]
        o_ref[...] = h
        hn_ref[...] = _rms_norm_f32(h, g2_ref[...]).astype(hn_ref.dtype)

    up = jnp.concatenate(_dot_chunks(hn_ref[...], wu_ref), axis=1)
    act = jnp.square(jnp.maximum(up, 0.0)).astype(BF16)
    for c, down in enumerate(_dot_chunks(act, wd_ref)):
        o_ref[:, c * COL_CHUNK:(c + 1) * COL_CHUNK] += down

    if final_norm:
        @pl.when(f == pl.num_programs(1) - 1)
        def _():
            o_ref[...] = _rms_norm_f32(o_ref[...], gf_ref[...])


def _out_mlp(x, a, c, w_out, b_out, g2, w_up, w_down, gf, tm, tf, final_norm):
    m, d = x.shape
    conv_ch = w_out.shape[1] - Q_DIM
    ff = w_down.shape[1]
    row = lambda i, f: (i, 0)
    return pl.pallas_call(
        functools.partial(_out_mlp_kernel, final_norm=final_norm),
        grid=(m // tm, ff // tf),
        in_specs=[pl.BlockSpec((tm, d), row), pl.BlockSpec((tm, Q_DIM), row),
                  pl.BlockSpec((tm, conv_ch), row),
                  _resident(w_out.shape), _resident((1, d)), _resident((1, d)),
                  pl.BlockSpec((tf // COL_CHUNK, d, COL_CHUNK), lambda i, f: (f, 0, 0)),
                  pl.BlockSpec((w_down.shape[0], tf, COL_CHUNK), lambda i, f: (0, f, 0)),
                  _resident((1, d))],
        out_specs=pl.BlockSpec((tm, d), row),
        out_shape=jax.ShapeDtypeStruct((m, d), F32),
        scratch_shapes=[pltpu.VMEM((tm, d), BF16)],
        compiler_params=_params("arbitrary", "arbitrary"),
        name="out_mlp",
    )(x, a, c, w_out, b_out, g2, w_up, w_down, gf)


def _rope_tables(pos):
    half = ROT_DIM // 2
    inv_freq = jnp.power(jnp.float32(ROPE_THETA), -jnp.arange(half, dtype=jnp.float32) * 2.0 / ROT_DIM)
    ang = pos.astype(jnp.float32)[:, None] * inv_freq[None, :]
    cos, sin = jnp.cos(ang), jnp.sin(ang)
    t = pos.shape[0]
    pad = jnp.zeros((t, HEAD_DIM - ROT_DIM), F32)
    zeros = jnp.zeros((t, half), F32)
    per_head = lambda a, b, fill: jnp.concatenate([a, b, pad + fill], axis=1)
    reps = LANES // HEAD_DIM
    cos_t = jnp.tile(per_head(cos, cos, 1.0), (1, reps))
    sup_t = jnp.tile(per_head(zeros, sin, 0.0), (1, reps))
    sdn_t = jnp.tile(per_head(-sin, zeros, 0.0), (1, reps))
    return cos_t, sup_t, sdn_t


def _heads_to_igd(a, axis):
    shape = a.shape
    a = a.reshape(shape[:axis] + (N_KV_HEADS, GROUP, HEAD_DIM) + shape[axis + 1:])
    return jnp.swapaxes(a, axis, axis + 1).reshape(shape)


def _cast_slabs_kernel(w_ref, o_ref):
    o_ref[0] = w_ref[...].astype(o_ref.dtype)


def _cast_slabs(w, tk):
    k, n = w.shape
    return pl.pallas_call(
        _cast_slabs_kernel,
        grid=(n // COL_CHUNK, k // tk),
        in_specs=[pl.BlockSpec((tk, COL_CHUNK), lambda j, kk: (kk, j))],
        out_specs=pl.BlockSpec((1, tk, COL_CHUNK), lambda j, kk: (j, kk, 0)),
        out_shape=jax.ShapeDtypeStruct((n // COL_CHUNK, k, COL_CHUNK), BF16),
        compiler_params=_params("arbitrary", "arbitrary"),
        name="cast_slabs",
    )(w)


def _igd_source(idx):
    i, g, d = idx // KV_DIM, (idx // HEAD_DIM) % N_KV_HEADS, idx % HEAD_DIM
    return (g * GROUP + i) * HEAD_DIM + d


def _prep_w_in_kernel(w_ref, o_ref):
    w = w_ref[...].astype(o_ref.dtype)
    src = lax.broadcasted_iota(jnp.int32, (Q_DIM, Q_DIM), 0)
    dst = lax.broadcasted_iota(jnp.int32, (Q_DIM, Q_DIM), 1)
    move = jnp.where(src == _igd_source(dst), 1.0, 0.0).astype(o_ref.dtype)
    o_ref[:, 0:Q_DIM] = _dot(w[:, 0:Q_DIM], move).astype(o_ref.dtype)
    o_ref[:, Q_DIM:] = w[:, Q_DIM:]


def _prep_w_in(w, tk):
    k, n = w.shape
    return pl.pallas_call(
        _prep_w_in_kernel,
        grid=(k // tk,),
        in_specs=[pl.BlockSpec((tk, n), lambda j: (j, 0))],
        out_specs=pl.BlockSpec((tk, n), lambda j: (j, 0)),
        out_shape=jax.ShapeDtypeStruct((k, n), BF16),
        compiler_params=_params("arbitrary"),
        name="prep_w_in",
    )(w)


def _prep_w_out_kernel(w_ref, o_ref):
    w = w_ref[...].astype(o_ref.dtype)
    dst = lax.broadcasted_iota(jnp.int32, (Q_DIM, Q_DIM), 0)
    src = lax.broadcasted_iota(jnp.int32, (Q_DIM, Q_DIM), 1)
    move = jnp.where(src == _igd_source(dst), 1.0, 0.0).astype(o_ref.dtype)
    o_ref[0, 0:Q_DIM, :] = _dot(move, w[0:Q_DIM, :]).astype(o_ref.dtype)
    o_ref[0, Q_DIM:, :] = w[Q_DIM:, :]


def _prep_w_out(w):
    k, n = w.shape
    return pl.pallas_call(
        _prep_w_out_kernel,
        grid=(n // COL_CHUNK,),
        in_specs=[pl.BlockSpec((k, COL_CHUNK), lambda c: (0, c))],
        out_specs=pl.BlockSpec((1, k, COL_CHUNK), lambda c: (c, 0, 0)),
        out_shape=jax.ShapeDtypeStruct((n // COL_CHUNK, k, COL_CHUNK), BF16),
        compiler_params=_params("arbitrary"),
        name="prep_w_out",
    )(w)


def _tile(m, target):
    t = min(m, target)
    assert m % t == 0, (m, t)
    return t


def kernel(x_prompt, x_sample, cache_k, cache_v, cache_conv, norm1_g, w_in, b_in, attn_sinks,
           conv_w, conv_b, conv_ln_g, conv_ln_b, w_out, b_out, norm2_g, w_up, w_down, final_norm_g):
    batch, seq, d_model = x_prompt.shape
    nsamp, t_s, _ = x_sample.shape
    depth = w_in.shape[0]
    conv_ch = conv_w.shape[2]
    wb = cache_k.shape[2]
    hist = CONV_W - 1
    assert t_s == 1 and seq % WINDOW == 0 and wb == WINDOW and cache_conv.shape[2] == hist
    assert depth >= 1

    tabs_p = _rope_tables(jnp.arange(seq, dtype=jnp.int32))
    tabs_s1 = _rope_tables(PAST_LEN + jnp.arange(t_s, dtype=jnp.int32))
    tabs_s = tuple(jnp.broadcast_to(t, (nsamp, LANES)) for t in tabs_s1)
    tabs_sc = tuple(jnp.tile(t[0], KV_DIM // LANES).reshape(KV_DIM, 1) for t in tabs_s1)
    row = lambda a: a.reshape(1, -1)

    hp = x_prompt.reshape(batch * seq, d_model)
    hs = x_sample.reshape(nsamp * t_s, d_model)
    tm_in = _tile(seq, 512)
    tm_mlp = _tile(batch * seq, 512)
    tf_s = 2 * COL_CHUNK
    nseq = _tile(nsamp, 8)
    o1, o2 = Q_DIM, Q_DIM + 2 * KV_DIM
    pk, pv, pc, sk, sv, sc = [], [], [], [], [], []
    for l in range(depth):
        last = l == depth - 1
        w_in_l = _prep_w_in(w_in[l], _tile(d_model, 512))
        b_in_l = row(jnp.concatenate([_heads_to_igd(b_in[l][:o1], 0), b_in[l][o1:]]))
        w_out_l = _prep_w_out(w_out[l])
        w_up_l, w_down_l = _cast_slabs(w_up[l], _tile(d_model, 2048)), _cast_slabs(w_down[l], 2048)
        sinks = attn_sinks[l].astype(F32)
        sink_rows = sinks.reshape(N_KV_HEADS, GROUP).T.reshape(N_HEADS, 1)
        g1, g2, gf = row(norm1_g[l]), row(norm2_g[l]), row(final_norm_g)
        cb, lg, lb, bo = row(conv_b[l]), row(conv_ln_g[l]), row(conv_ln_b[l]), row(b_out[l])

        q, k, v, u, pk_l, pv_l, pc_l = _in_proj(hp, g1, w_in_l, b_in_l, *tabs_p, tm_in, batch)
        by_part = lambda a: a.reshape(a.shape[0], PARTS, KV_DIM).transpose(1, 0, 2)
        hp = _mlp_mix(hp, q, k, v, u, sinks, by_part(conv_w[l]), by_part(cb), lg, lb, w_out_l, bo, g2, gf,
                      w_up_l, w_down_l, seq, tm_mlp, 1, last)
        pk.append(pk_l.reshape(batch, WINDOW, N_KV_HEADS, HEAD_DIM))
        pv.append(pv_l.reshape(batch, WINDOW, N_KV_HEADS, HEAD_DIM))
        pc.append(pc_l)

        cache_kt = jnp.transpose(cache_k[l], (0, 2, 3, 1)).reshape(nsamp, KV_DIM, wb)
        cache_vt = jnp.transpose(cache_v[l], (0, 2, 3, 1)).reshape(nsamp, KV_DIM, wb)
        cache_ct = jnp.transpose(cache_conv[l], (1, 0, 2))
        bkv_c = b_in[l][o1:o2].reshape(2 * KV_DIM, 1)
        q, kt, vt, u = _in_proj_sample(hs, g1, w_in_l, b_in_l, bkv_c, tabs_s, tabs_sc)
        q_rep = jnp.repeat(q.reshape(nsamp, GROUP, KV_DIM), N_KV_HEADS, axis=1)
        by_block = lambda a: a.reshape(KV_DIM, nsamp // nseq, nseq).transpose(1, 0, 2)
        a8, nkt, nvt = _attn_sample(q_rep, by_block(kt), by_block(vt), cache_kt, cache_vt, sink_rows, nseq)
        c_o, nct = _conv_sample(cache_ct, u, conv_w[l], cb, lg, lb, 5)
        a_o = a8[:, :GROUP, :].reshape(nsamp, Q_DIM).astype(BF16)
        hs = _out_mlp(hs, a_o, c_o, w_out_l, bo, g2, w_up_l, w_down_l, gf, nsamp, tf_s, last)
        to_cache = lambda a: jnp.transpose(a.reshape(nsamp, N_KV_HEADS, HEAD_DIM, wb), (0, 3, 1, 2))
        sk.append(to_cache(nkt))
        sv.append(to_cache(nvt))
        sc.append(jnp.transpose(nct, (1, 0, 2)))

    y_prompt = hp.reshape(batch, seq, d_model)
    y_sample = hs.reshape(nsamp, t_s, d_model)
    return (y_prompt, y_sample, jnp.stack(pk), jnp.stack(pv), jnp.stack(pc),
            jnp.stack(sk), jnp.stack(sv), jnp.stack(sc))
```
